```python
import jax, jax.numpy as jnp
from jax import lax
import numpy as np


D_MODEL = 1024
BATCH = 2
SEQ = 8192
DEPTH = 2
DEC_BATCH = 1
DEC_SEQ = 16384
PAST_LEN = 128

GRID_W = 64
N_META = 16
NA_HEADS = 8
NA_HEAD_DIM = 64
NA_WIN_H = 8
NA_WIN_W = 16
NA_WIDTH = NA_HEADS * NA_HEAD_DIM
MLA_HEADS = 8
QK_NOPE = 64
QK_ROPE = 32
V_HEAD = 64
Q_LORA = 256
KV_LORA = 128
ROPE_THETA = 10000.0
MLA_WIDTH = MLA_HEADS * V_HEAD
Q_BLOCK = 128
IN_COLS = 3 * NA_WIDTH + Q_LORA + KV_LORA + QK_ROPE
MIX_WIDTH = NA_WIDTH + MLA_WIDTH
N_EXPERTS = 32
TOP_K = 4
D_FF = 1024
SWIGLU_LIMIT = 7.0
SWIGLU_ALPHA = 1.702
EXPERT_BLOCK = 128
EPS = 1e-6

kernel_name = 'hybrid_natten_mla_moe_encoder'


def _rmsnorm(x, g):
    x32 = x.astype(jnp.float32)
    y = x32 * lax.rsqrt(jnp.mean(x32 * x32, axis=-1, keepdims=True) + EPS)
    return (y * g.astype(jnp.float32)).astype(x.dtype)


def _rope(x, pos):
    r = x.shape[-1]
    freqs = jnp.power(ROPE_THETA, -jnp.arange(0, r, 2, dtype=jnp.float32) / r)
    ang = pos[:, None] * freqs[None, :]
    cos = jnp.cos(ang)[None, :, None, :].astype(x.dtype)
    sin = jnp.sin(ang)[None, :, None, :].astype(x.dtype)
    x1, x2 = x[..., : r // 2], x[..., r // 2:]
    return jnp.concatenate([x1 * cos - x2 * sin, x2 * cos + x1 * sin], axis=-1)


def _dense_attention(q, k, v, scale):
    s = jnp.einsum('bqhd,bkhd->bhqk', q, k).astype(jnp.float32) * scale
    p = jax.nn.softmax(s, axis=-1).astype(v.dtype)
    return jnp.einsum('bhqk,bkhd->bqhd', p, v)


def _neighbourhood_attention(q, k, v, rpb_l):
    bx, lx, nh, dh = q.shape
    n_tok = lx - N_META
    rows = n_tok // GRID_W
    kh = min(NA_WIN_H, rows)
    kw = NA_WIN_W
    scale = dh ** -0.5
    qm, km, vm = q[:, :N_META], k[:, :N_META], v[:, :N_META]
    out_meta = _dense_attention(qm, km, vm, scale)
    qg = q[:, N_META:].reshape(bx, rows, GRID_W, nh, dh)
    kg = k[:, N_META:].reshape(bx, rows, GRID_W, nh, dh)
    vg = v[:, N_META:].reshape(bx, rows, GRID_W, nh, dh)
    cols = jnp.arange(GRID_W)
    col_start = jnp.clip(cols - kw // 2, 0, GRID_W - kw)
    col_idx = col_start[:, None] + jnp.arange(kw)[None, :]
    dc_idx = col_idx - cols[:, None] + (kw - 1)

    def row_fn(r):
        rs = jnp.clip(r - kh // 2, 0, rows - kh)
        kb = lax.dynamic_slice_in_dim(kg, rs, kh, axis=1)[:, :, col_idx]
        vb = lax.dynamic_slice_in_dim(vg, rs, kh, axis=1)[:, :, col_idx]
        qr = lax.dynamic_index_in_dim(qg, r, axis=1, keepdims=False)
        s_loc = jnp.einsum('bchd,bkcwhd->bhckw', qr, kb).astype(jnp.float32) * scale
        dr_idx = rs + jnp.arange(kh) - r + (NA_WIN_H - 1)
        bias = rpb_l[:, dr_idx][:, :, dc_idx]
        s_loc = s_loc + jnp.transpose(bias, (0, 2, 1, 3)).astype(jnp.float32)[None]
        s_met = jnp.einsum('bchd,bmhd->bhcm', qr, km).astype(jnp.float32) * scale
        s = jnp.concatenate([s_loc.reshape(bx, nh, GRID_W, kh * kw), s_met], axis=-1)
        p = jax.nn.softmax(s, axis=-1).astype(v.dtype)
        p_loc = p[..., : kh * kw].reshape(bx, nh, GRID_W, kh, kw)
        p_met = p[..., kh * kw:]
        return (jnp.einsum('bhckw,bkcwhd->bchd', p_loc, vb)
                + jnp.einsum('bhcm,bmhd->bchd', p_met, vm))

    out_rows = lax.map(row_fn, jnp.arange(rows))
    out_tok = jnp.moveaxis(out_rows, 0, 1).reshape(bx, n_tok, nh, dh)
    return jnp.concatenate([out_meta, out_tok], axis=1).reshape(bx, lx, nh * dh)


def _latent_attention(c_q, c_kv, k_rope_raw, g_q, w_uq, g_kv, w_ukv, pos):
    bx, lx, _ = c_q.shape
    dq = QK_NOPE + QK_ROPE
    q = (_rmsnorm(c_q, g_q) @ w_uq).reshape(bx, lx, MLA_HEADS, dq)
    q = jnp.concatenate([q[..., :QK_NOPE], _rope(q[..., QK_NOPE:], pos)], axis=-1)
    kv = (_rmsnorm(c_kv, g_kv) @ w_ukv).reshape(bx, lx, MLA_HEADS, QK_NOPE + V_HEAD)
    k_pe = _rope(k_rope_raw[:, :, None, :], pos)
    k = jnp.concatenate([kv[..., :QK_NOPE],
                         jnp.broadcast_to(k_pe, (bx, lx, MLA_HEADS, QK_ROPE))], axis=-1)
    v = kv[..., QK_NOPE:]
    scale = dq ** -0.5
    out_meta = _dense_attention(q[:, :N_META], k, v, scale)
    n_tok = lx - N_META
    qb = q[:, N_META:].reshape(bx, n_tok // Q_BLOCK, Q_BLOCK, MLA_HEADS, dq)
    qb = jnp.transpose(qb, (1, 0, 2, 3, 4))
    ob = lax.map(lambda qq: _dense_attention(qq, k, v, scale), qb)
    out_tok = jnp.transpose(ob, (1, 0, 2, 3, 4)).reshape(bx, n_tok, MLA_HEADS, V_HEAD)
    return jnp.concatenate([out_meta, out_tok], axis=1).reshape(bx, lx, MLA_WIDTH)


def _moe(x, w_router, b_router, w_up, b_up, w_down, b_down):
    bx, lx, dm = x.shape
    t = bx * lx
    xt = x.reshape(t, dm)
    logits = (xt @ w_router + b_router).astype(jnp.float32)
    top_val, top_idx = lax.top_k(logits, TOP_K)
    gates = jax.nn.softmax(top_val, axis=-1)
    a = t * TOP_K
    flat_e = top_idx.reshape(-1)
    flat_tok = jnp.arange(a) // TOP_K
    order = jnp.argsort(flat_e)
    sorted_e = flat_e[order]
    counts = jnp.bincount(flat_e, length=N_EXPERTS)
    padded = (counts + EXPERT_BLOCK - 1) // EXPERT_BLOCK * EXPERT_BLOCK
    start = jnp.cumsum(counts) - counts
    pend = jnp.cumsum(padded)
    pstart = pend - padded
    dest = pstart[sorted_e] + jnp.arange(a) - start[sorted_e]
    n_blocks = -(-a // EXPERT_BLOCK) + N_EXPERTS
    p_rows = n_blocks * EXPERT_BLOCK
    tok_buf = jnp.full((p_rows,), t, dtype=jnp.int32).at[dest].set(flat_tok[order].astype(jnp.int32))
    gate_buf = jnp.zeros((p_rows,), jnp.float32).at[dest].set(gates.reshape(-1)[order])
    block_e = jnp.clip(jnp.searchsorted(pend, jnp.arange(n_blocks) * EXPERT_BLOCK, side='right'),
                       0, N_EXPERTS - 1)
    x_pad = jnp.concatenate([xt, jnp.zeros((1, dm), xt.dtype)], axis=0)
    xb = x_pad[tok_buf].reshape(n_blocks, EXPERT_BLOCK, dm)

    def expert_block(args):
        xblk, e = args
        h = xblk @ w_up[e] + b_up[e]
        gate = jnp.minimum(h[:, :D_FF], SWIGLU_LIMIT)
        up = jnp.clip(h[:, D_FF:], -SWIGLU_LIMIT, SWIGLU_LIMIT)
        glu = gate * jax.nn.sigmoid(gate * SWIGLU_ALPHA)
        return ((up + 1.0) * glu) @ w_down[e] + b_down[e]

    yb = lax.map(expert_block, (xb, block_e)).reshape(p_rows, dm)
    out = jnp.zeros((t + 1, dm), yb.dtype).at[tok_buf].add(yb * gate_buf[:, None].astype(yb.dtype))
    return out[:t].reshape(bx, lx, dm)


def _trunk(x, meta_tokens, g_attn, w_in, g_q, w_uq, g_kv, w_ukv, rpb, g_out_na, g_out_mla,
           w_out, g_ffn, w_router, b_router, w_up, b_up, w_down, b_down, g_final):
    bx = x.shape[0]
    meta = jnp.broadcast_to(meta_tokens[None].astype(x.dtype), (bx, N_META, D_MODEL))
    h = jnp.concatenate([meta, x], axis=1)
    lx = h.shape[1]
    pos = jnp.arange(lx, dtype=jnp.float32)
    s0 = NA_WIDTH
    s1 = 2 * NA_WIDTH
    s2 = 3 * NA_WIDTH
    s3 = s2 + Q_LORA
    s4 = s3 + KV_LORA
    for l in range(DEPTH):
        a = _rmsnorm(h, g_attn[l])
        proj = a @ w_in[l]
        q_na = proj[..., :s0].reshape(bx, lx, NA_HEADS, NA_HEAD_DIM)
        k_na = proj[..., s0:s1].reshape(bx, lx, NA_HEADS, NA_HEAD_DIM)
        v_na = proj[..., s1:s2].reshape(bx, lx, NA_HEADS, NA_HEAD_DIM)
        o_na = _neighbourhood_attention(q_na, k_na, v_na, rpb[l])
        o_mla = _latent_attention(proj[..., s2:s3], proj[..., s3:s4], proj[..., s4:],
                                  g_q[l], w_uq[l], g_kv[l], w_ukv[l], pos)
        mixed = jnp.concatenate([_rmsnorm(o_na, g_out_na[l]), _rmsnorm(o_mla, g_out_mla[l])], axis=-1)
        h = h + mixed @ w_out[l]
        m = _rmsnorm(h, g_ffn[l])
        h = h + _moe(m, w_router[l], b_router[l], w_up[l], b_up[l], w_down[l], b_down[l])
    h = _rmsnorm(h, g_final)
    return h[:, N_META:]


def setup_inputs(seed: int = 0) -> dict:
    key = jax.random.key(seed)
    ks = jax.random.split(key, 24)
    f32 = jnp.float32

    def nrm(k, shape, scale):
        return jax.random.normal(k, shape, f32) * scale

    def gain(k, shape):
        return 1.0 + 0.01 * jax.random.normal(k, shape, f32)

    return {
        'x_prompt': nrm(ks[0], (BATCH, SEQ, D_MODEL), 1.0),
        'x_sample': nrm(ks[1], (DEC_BATCH, DEC_SEQ, D_MODEL), 1.0),
        'meta_tokens': nrm(ks[2], (N_META, D_MODEL), 1.0),
        'g_attn': gain(ks[3], (DEPTH, D_MODEL)),
        'w_in': nrm(ks[4], (DEPTH, D_MODEL, IN_COLS), D_MODEL ** -0.5),
        'g_q': gain(ks[5], (DEPTH, Q_LORA)),
        'w_uq': nrm(ks[6], (DEPTH, Q_LORA, MLA_HEADS * (QK_NOPE + QK_ROPE)), Q_LORA ** -0.5),
        'g_kv': gain(ks[7], (DEPTH, KV_LORA)),
        'w_ukv': nrm(ks[8], (DEPTH, KV_LORA, MLA_HEADS * (QK_NOPE + V_HEAD)), KV_LORA ** -0.5),
        'rpb': nrm(ks[9], (DEPTH, NA_HEADS, 2 * NA_WIN_H - 1, 2 * NA_WIN_W - 1), 0.02),
        'g_out_na': gain(ks[10], (DEPTH, NA_WIDTH)),
        'g_out_mla': gain(ks[11], (DEPTH, MLA_WIDTH)),
        'w_out': nrm(ks[12], (DEPTH, MIX_WIDTH, D_MODEL), MIX_WIDTH ** -0.5),
        'g_ffn': gain(ks[13], (DEPTH, D_MODEL)),
        'w_router': nrm(ks[14], (DEPTH, D_MODEL, N_EXPERTS), D_MODEL ** -0.5),
        'b_router': nrm(ks[15], (DEPTH, N_EXPERTS), 0.01),
        'w_up': nrm(ks[16], (DEPTH, N_EXPERTS, D_MODEL, 2 * D_FF), D_MODEL ** -0.5),
        'b_up': nrm(ks[17], (DEPTH, N_EXPERTS, 2 * D_FF), 0.01),
        'w_down': nrm(ks[18], (DEPTH, N_EXPERTS, D_FF, D_MODEL), D_FF ** -0.5),
        'b_down': nrm(ks[19], (DEPTH, N_EXPERTS, D_MODEL), 0.01),
        'g_final': gain(ks[20], (D_MODEL,)),
    }


def reference(x_prompt, x_sample, meta_tokens, g_attn, w_in, g_q, w_uq, g_kv, w_ukv, rpb,
              g_out_na, g_out_mla, w_out, g_ffn, w_router, b_router, w_up, b_up, w_down,
              b_down, g_final):
    y_prompt = _trunk(x_prompt, meta_tokens, g_attn, w_in, g_q, w_uq, g_kv, w_ukv, rpb,
                      g_out_na, g_out_mla, w_out, g_ffn, w_router, b_router, w_up, b_up,
                      w_down, b_down, g_final)
    y_sample = _trunk(x_sample, meta_tokens, g_attn, w_in, g_q, w_uq, g_kv, w_ukv, rpb,
                      g_out_na, g_out_mla, w_out, g_ffn, w_router, b_router, w_up, b_up,
                      w_down, b_down, g_final)
    return (y_prompt, y_sample)
```

```python
import functools

import numpy as np
import jax
import jax.numpy as jnp
from jax import lax
from jax.experimental import pallas as pl
from jax.experimental.pallas import tpu as pltpu

D_MODEL = 1024
GRID_W = 64
N_META = 16
NA_HEADS = 8
NA_HEAD_DIM = 64
NA_WIN_H = 8
NA_WIN_W = 16
NA_WIDTH = NA_HEADS * NA_HEAD_DIM
MLA_HEADS = 8
QK_NOPE = 64
QK_ROPE = 32
V_HEAD = 64
Q_LORA = 256
KV_LORA = 128
ROPE_THETA = 10000.0
MLA_WIDTH = MLA_HEADS * V_HEAD
N_EXPERTS = 32
TOP_K = 4
D_FF = 1024
SWIGLU_LIMIT = 7.0
SWIGLU_ALPHA = 1.702
EPS = 1e-6

LANES = 128
HEAD_TILE = LANES
N_PAIRS = NA_HEADS // 2
ROW_BLOCK = 256
NA_QROWS = 4
NA_QBLOCK = NA_QROWS * GRID_W
NA_KBLOCKS = 3
EXPERT_BLOCK = 256
NEG_BIG = -1e30
VMEM_LIMIT = 56 * 1024 * 1024

_F32 = jnp.float32
_BF16 = jnp.bfloat16


def _cparams(sem):
    return pltpu.CompilerParams(dimension_semantics=sem, vmem_limit_bytes=VMEM_LIMIT)


def _rms(x, g):
    return x * lax.rsqrt(jnp.mean(x * x, axis=-1, keepdims=True) + EPS) * g


def _dot(a, b):
    return jnp.dot(a, b, preferred_element_type=_F32)


def _dot_nt(a, b):
    return lax.dot_general(a, b, (((1,), (1,)), ((), ())), preferred_element_type=_F32)


def _attn_in_kernel(h_ref, cos_ref, sin_ref, g_attn_ref, w_in_ref, g_q_ref, w_q_ref, g_kv_ref,
                    w_kv_ref, w_kr_ref, vone_ref,
                    qna_ref, kna_ref, vna_ref, qm_ref, km_ref, vm_ref):
    a = _rms(h_ref[...], g_attn_ref[...]).astype(_BF16)
    proj = _dot(a, w_in_ref[...])
    qna_ref[...] = proj[:, 0:NA_WIDTH].astype(_BF16)
    kna_ref[...] = proj[:, NA_WIDTH:2 * NA_WIDTH].astype(_BF16)
    vna_ref[...] = proj[:, 2 * NA_WIDTH:3 * NA_WIDTH].astype(_BF16)
    s2 = 3 * NA_WIDTH
    cq = proj[:, s2:s2 + Q_LORA]
    ckv = proj[:, s2 + Q_LORA:s2 + Q_LORA + KV_LORA]
    kr = proj[:, s2 + Q_LORA + KV_LORA:]
    cos = jnp.concatenate([cos_ref[...]] * MLA_HEADS, axis=1)
    sin = jnp.concatenate([sin_ref[...]] * MLA_HEADS, axis=1)
    width = MLA_HEADS * HEAD_TILE
    q2 = _dot(_rms(cq, g_q_ref[...]).astype(_BF16), w_q_ref[...])
    scale = (QK_NOPE + QK_ROPE) ** -0.5
    qm_ref[...] = ((q2[:, :width] * cos + q2[:, width:] * sin) * scale).astype(_BF16)
    kv2 = _dot(_rms(ckv, g_kv_ref[...]).astype(_BF16), w_kv_ref[...])
    kr2 = _dot(kr.astype(_BF16), w_kr_ref[...])
    km_ref[...] = ((kv2[:, :width] + kr2[:, :width]) * cos + kr2[:, width:] * sin).astype(_BF16)
    vm_ref[...] = (kv2[:, width:] + vone_ref[...]).astype(_BF16)


def _attn_in(h, cos_t, sin_t, g_attn, w_in_p, g_q, w_q_p, g_kv, w_kv_p, w_kr_p, vone):
    rt = h.shape[0]
    tb = ROW_BLOCK
    width = MLA_HEADS * HEAD_TILE
    row = lambda w: pl.BlockSpec((tb, w), lambda i: (i, 0))
    full = lambda a: pl.BlockSpec(a.shape, lambda i: (0,) * a.ndim)
    outs = [jax.ShapeDtypeStruct((rt, NA_WIDTH), _BF16)] * 3 + [jax.ShapeDtypeStruct((rt, width), _BF16)] * 3
    return pl.pallas_call(
        _attn_in_kernel,
        grid=(rt // tb,),
        in_specs=[row(D_MODEL), row(LANES), row(LANES), full(g_attn), full(w_in_p), full(g_q), full(w_q_p),
                  full(g_kv), full(w_kv_p), full(w_kr_p), full(vone)],
        out_specs=[row(NA_WIDTH)] * 3 + [row(width)] * 3,
        out_shape=outs,
        compiler_params=_cparams(("parallel",)),
        name="attn_in",
    )(h, cos_t, sin_t, g_attn, w_in_p, g_q, w_q_p, g_kv, w_kv_p, w_kr_p, vone)


def _na_kernel(cidx_ref, var_ref, midx_ref, q_ref, kp_ref, kc_ref, kn_ref, vp_ref, vc_ref, vn_ref,
               km_ref, vm_ref, bias_ref, oin_ref, o_ref):
    del cidx_ref, var_ref, midx_ref, oin_ref
    q = q_ref[...]
    lane = lax.broadcasted_iota(jnp.int32, (1, LANES), 1)
    ks = (kp_ref[...], kc_ref[...], kn_ref[...])
    vs = (vp_ref[...], vc_ref[...], vn_ref[...])
    km = km_ref[...]
    vm = vm_ref[...]
    outs = []
    for hh in range(2):
        in_head = (lane >= hh * NA_HEAD_DIM) & (lane < (hh + 1) * NA_HEAD_DIM)
        qh = jnp.where(in_head, q, jnp.zeros_like(q))
        s_loc = jnp.concatenate([_dot_nt(qh, k) for k in ks], axis=1) + bias_ref[0, hh]
        s_met = _dot_nt(qh, km)
        m = jnp.maximum(jnp.max(s_loc, axis=-1, keepdims=True), jnp.max(s_met, axis=-1, keepdims=True))
        p_loc = jnp.exp(s_loc - m)
        p_met = jnp.exp(s_met - m)
        l = jnp.sum(p_loc, axis=-1, keepdims=True) + jnp.sum(p_met, axis=-1, keepdims=True)
        o = _dot(p_met.astype(_BF16), vm)
        for j in range(NA_KBLOCKS):
            o = o + _dot(p_loc[:, j * NA_QBLOCK:(j + 1) * NA_QBLOCK].astype(_BF16), vs[j])
        outs.append(o / l)
    o_ref[...] = jnp.where(lane < NA_HEAD_DIM, outs[0], outs[1])


def _na_meta_kernel(midx_ref, q_ref, k_ref, v_ref, oin_ref, o_ref):
    del midx_ref, oin_ref
    q = q_ref[...]
    k = k_ref[...]
    v = v_ref[...]
    lane = lax.broadcasted_iota(jnp.int32, (1, LANES), 1)
    outs = []
    for hh in range(2):
        in_head = (lane >= hh * NA_HEAD_DIM) & (lane < (hh + 1) * NA_HEAD_DIM)
        s = _dot_nt(jnp.where(in_head, q, jnp.zeros_like(q)), k)
        p = jnp.exp(s - jnp.max(s, axis=-1, keepdims=True))
        outs.append(_dot(p.astype(_BF16), v) / jnp.sum(p, axis=-1, keepdims=True))
    o_ref[...] = jnp.where(lane < NA_HEAD_DIM, outs[0], outs[1])


def _na_bias(rpb_l):
    qr = np.arange(NA_QROWS)[:, None, None, None]
    qc = np.arange(GRID_W)[None, :, None, None]
    kr = np.arange(NA_KBLOCKS * NA_QROWS)[None, None, :, None]
    kc = np.arange(GRID_W)[None, None, None, :]
    cs = np.clip(qc - NA_WIN_W // 2, 0, GRID_W - NA_WIN_W)
    col_ok = (kc >= cs) & (kc < cs + NA_WIN_W)
    dc = np.clip(kc - qc + NA_WIN_W - 1, 0, 2 * NA_WIN_W - 2)
    variants = []
    nrows = 4 * NA_KBLOCKS * NA_QROWS
    for r0, k0 in ((0, 0), (NA_QROWS, 0), (nrows - NA_QROWS, nrows - NA_KBLOCKS * NA_QROWS)):
        r = r0 + qr
        key_row = k0 + kr
        rs = np.clip(r - NA_WIN_H // 2, 0, nrows - NA_WIN_H)
        row_ok = (key_row >= rs) & (key_row < rs + NA_WIN_H)
        dr = np.clip(key_row - r + NA_WIN_H - 1, 0, 2 * NA_WIN_H - 2)
        ok = np.broadcast_to(row_ok & col_ok, (NA_QROWS, GRID_W, NA_KBLOCKS * NA_QROWS, GRID_W))
        dr_b = np.broadcast_to(dr, ok.shape)
        dc_b = np.broadcast_to(dc, ok.shape)
        b = jnp.where(ok[None], rpb_l[:, dr_b, dc_b].astype(_F32), NEG_BIG)
        variants.append(b.reshape(NA_HEADS, NA_QBLOCK, NA_KBLOCKS * NA_QBLOCK))
    return jnp.stack(variants, axis=0)


def _na_attention(q, k, v, bias, layout):
    rt = q.shape[0]
    cidx, var, midx = layout["na_cidx"], layout["na_var"], layout["na_midx"]
    nblk = cidx.shape[0]
    qspec = pl.BlockSpec((NA_QBLOCK, LANES), lambda p, b, c, vr, m: (b, p))
    kspec = lambda d: pl.BlockSpec((NA_QBLOCK, LANES), lambda p, b, c, vr, m: (c[b] + d, p))
    mspec = pl.BlockSpec((N_META, LANES), lambda p, b, c, vr, m: (m[b], p))
    bspec = pl.BlockSpec((1, 2, NA_QBLOCK, NA_KBLOCKS * NA_QBLOCK), lambda p, b, c, vr, m: (vr[b], p, 0, 0))
    o = pl.pallas_call(
        _na_kernel,
        grid_spec=pltpu.PrefetchScalarGridSpec(
            num_scalar_prefetch=3,
            grid=(N_PAIRS, nblk),
            in_specs=[qspec, kspec(-1), kspec(0), kspec(1), kspec(-1), kspec(0), kspec(1), mspec, mspec, bspec,
                      pl.BlockSpec(memory_space=pl.ANY)],
            out_specs=qspec),
        out_shape=jax.ShapeDtypeStruct((rt, NA_WIDTH), _F32),
        input_output_aliases={13: 0},
        compiler_params=_cparams(("parallel", "parallel")),
        name="na_attn",
    )(cidx, var, midx, q, k, k, k, v, v, v, k, v, bias, jnp.zeros((rt, NA_WIDTH), _F32))
    smidx = layout["seq_midx"]
    mq = pl.BlockSpec((N_META, LANES), lambda p, s, m: (m[s], p))
    return pl.pallas_call(
        _na_meta_kernel,
        grid_spec=pltpu.PrefetchScalarGridSpec(
            num_scalar_prefetch=1,
            grid=(N_PAIRS, smidx.shape[0]),
            in_specs=[mq, mq, mq, pl.BlockSpec(memory_space=pl.ANY)],
            out_specs=mq),
        out_shape=jax.ShapeDtypeStruct((rt, NA_WIDTH), _F32),
        input_output_aliases={4: 0},
        compiler_params=_cparams(("parallel", "parallel")),
        name="na_meta",
    )(smidx, q, k, v, o)


def _mla_kernel(q_ref, k_ref, v_ref, km_ref, vm_ref, oin_ref, o_ref, m_sc, acc_sc):
    del oin_ref
    t = pl.program_id(3)
    nt = pl.num_programs(3)

    @pl.when(t == 0)
    def _():
        m_sc[...] = jnp.full(m_sc.shape, NEG_BIG, _F32)
        acc_sc[...] = jnp.zeros(acc_sc.shape, _F32)

    def update(hh, k, v):
        q = q_ref[:, hh * HEAD_TILE:(hh + 1) * HEAD_TILE]
        s = _dot_nt(q, k)
        m_prev = m_sc[hh]
        m_new = jnp.maximum(m_prev, jnp.max(s, axis=-1, keepdims=True))
        p = jnp.exp(s - m_new)
        acc_sc[hh] = jnp.exp(m_prev - m_new) * acc_sc[hh] + _dot(p.astype(_BF16), v)
        m_sc[hh] = m_new

    for hh in range(2):
        update(hh, k_ref[:, hh * HEAD_TILE:(hh + 1) * HEAD_TILE], v_ref[:, hh * HEAD_TILE:(hh + 1) * HEAD_TILE])

    @pl.when(t == nt - 1)
    def _():
        for hh in range(2):
            update(hh, km_ref[:, hh * HEAD_TILE:(hh + 1) * HEAD_TILE],
                   vm_ref[:, hh * HEAD_TILE:(hh + 1) * HEAD_TILE])
        lane = lax.broadcasted_iota(jnp.int32, (1, LANES), 1)
        acc0 = acc_sc[0]
        acc1 = acc_sc[1]
        l0 = acc0[:, V_HEAD:V_HEAD + 1]
        l1 = acc1[:, 0:1]
        o_ref[...] = jnp.where(lane < V_HEAD, acc0 / l0, acc1 / l1)


def _mla_call(q, k, v, o_prev, *, tq, tk, n_seq, q_blk0, q_blk_stride, n_qblk, kv_blk0, kv_blk_stride, n_kvblk,
              meta_blk0, name):
    rt = q.shape[0]
    pw = 2 * HEAD_TILE
    qspec = pl.BlockSpec((tq, pw), lambda s, p, i, t: (q_blk0 + s * q_blk_stride + i, p))
    kspec = pl.BlockSpec((tk, pw), lambda s, p, i, t: (kv_blk0 + s * kv_blk_stride + t, p))
    mspec = pl.BlockSpec((N_META, pw), lambda s, p, i, t: (meta_blk0 + s, p))
    ospec = pl.BlockSpec((tq, LANES), lambda s, p, i, t: (q_blk0 + s * q_blk_stride + i, p))
    return pl.pallas_call(
        _mla_kernel,
        grid=(n_seq, N_PAIRS, n_qblk, n_kvblk),
        in_specs=[qspec, kspec, kspec, mspec, mspec, pl.BlockSpec(memory_space=pl.ANY)],
        out_specs=ospec,
        out_shape=jax.ShapeDtypeStruct((rt, MLA_WIDTH), _F32),
        scratch_shapes=[pltpu.VMEM((2, tq, 1), _F32), pltpu.VMEM((2, tq, LANES), _F32)],
        input_output_aliases={5: 0},
        compiler_params=_cparams(("parallel", "parallel", "parallel", "arbitrary")),
        name=name,
    )(q, k, v, k, v, o_prev)


def _mla_attention(q, k, v, layout):
    rt = q.shape[0]
    o = jnp.zeros((rt, MLA_WIDTH), _F32)
    nt = layout["n_tok_total"]
    for gi, (seq0, n_seq, n_tok) in enumerate(layout["groups"]):
        tq = tk = min(512, n_tok)
        off = layout["tok_off"][seq0]
        o = _mla_call(q, k, v, o, tq=tq, tk=tk, n_seq=n_seq, q_blk0=off // tq, q_blk_stride=n_tok // tq,
                      n_qblk=n_tok // tq, kv_blk0=off // tk, kv_blk_stride=n_tok // tk, n_kvblk=n_tok // tk,
                      meta_blk0=nt // N_META + seq0, name=f"mla_tok{gi}")
        o = _mla_call(q, k, v, o, tq=N_META, tk=tk, n_seq=n_seq, q_blk0=nt // N_META + seq0, q_blk_stride=1,
                      n_qblk=1, kv_blk0=off // tk, kv_blk_stride=n_tok // tk, n_kvblk=n_tok // tk,
                      meta_blk0=nt // N_META + seq0, name=f"mla_meta{gi}")
    return o


def _attn_out_kernel(ona_ref, omla_ref, h_ref, g_na_ref, g_mla_ref, w_out_ref, g_ffn_ref, w_r_ref, b_r_ref,
                     h1_ref, m_ref, idx_ref, pos_ref, gate_ref, cnt_ref, cnt_sc, *, n_valid):
    i = pl.program_id(0)
    tb = h_ref.shape[0]

    @pl.when(i == 0)
    def _():
        cnt_sc[...] = jnp.zeros(cnt_sc.shape, _F32)

    n1 = _rms(ona_ref[...], g_na_ref[...]).astype(_BF16)
    n2 = _rms(omla_ref[...], g_mla_ref[...]).astype(_BF16)
    h1 = h_ref[...] + _dot(n1, w_out_ref[0:NA_WIDTH, :]) + _dot(n2, w_out_ref[NA_WIDTH:, :])
    h1_ref[...] = h1
    m = _rms(h1, g_ffn_ref[...])
    m_ref[...] = m
    logits = jnp.dot(m, w_r_ref[...], preferred_element_type=_F32, precision=lax.Precision.HIGHEST) + b_r_ref[...]

    lane = lax.broadcasted_iota(jnp.int32, (tb, LANES), 1).astype(_F32)
    row = lax.broadcasted_iota(jnp.int32, (tb, 1), 0) + i * tb
    valid = jnp.where(row < n_valid, 1.0, 0.0)
    work = logits
    sel = jnp.zeros((tb, LANES), _F32)
    idx_out = jnp.zeros((tb, LANES), _F32)
    top = []
    for kk in range(TOP_K):
        mx = jnp.max(work, axis=-1, keepdims=True)
        idx = jnp.min(jnp.where(work == mx, lane, float(LANES)), axis=-1, keepdims=True)
        hit = lane == idx
        sel = jnp.where(hit, 1.0, sel)
        work = jnp.where(hit, NEG_BIG * 2, work)
        idx_out = jnp.where(lane == kk, idx, idx_out)
        top.append((mx, idx))
    e = [jnp.exp(mx - top[0][0]) for mx, _ in top]
    denom = e[0] + e[1] + e[2] + e[3]
    gate_out = jnp.zeros((tb, LANES), _F32)
    for kk in range(TOP_K):
        gate_out = jnp.where(lane == kk, e[kk] / denom, gate_out)
    gate_ref[...] = gate_out * valid
    idx_ref[...] = idx_out

    sel = sel * valid
    r_i = lax.broadcasted_iota(jnp.int32, (tb, tb), 0)
    c_i = lax.broadcasted_iota(jnp.int32, (tb, tb), 1)
    tri = jnp.where(c_i < r_i, 1.0, 0.0).astype(_BF16)
    pos_full = _dot(tri, sel.astype(_BF16)) + cnt_sc[...]
    pos_out = jnp.zeros((tb, LANES), _F32)
    for kk in range(TOP_K):
        pk = jnp.sum(jnp.where(lane == top[kk][1], pos_full, 0.0), axis=-1, keepdims=True)
        pos_out = jnp.where(lane == kk, pk, pos_out)
    pos_ref[...] = pos_out
    cnt_sc[...] = cnt_sc[...] + jnp.sum(sel, axis=0, keepdims=True)
    cnt_ref[...] = cnt_sc[...]


def _attn_out(o_na, o_mla, h, g_na, g_mla, w_out, g_ffn, w_r, b_r, n_valid):
    rt = h.shape[0]
    tb = ROW_BLOCK
    row = lambda w: pl.BlockSpec((tb, w), lambda i: (i, 0))
    full = lambda a: pl.BlockSpec(a.shape, lambda i: (0,) * a.ndim)
    return pl.pallas_call(
        functools.partial(_attn_out_kernel, n_valid=n_valid),
        grid=(rt // tb,),
        in_specs=[row(NA_WIDTH), row(MLA_WIDTH), row(D_MODEL), full(g_na), full(g_mla), full(w_out), full(g_ffn),
                  full(w_r), full(b_r)],
        out_specs=[row(D_MODEL), row(D_MODEL), row(LANES), row(LANES), row(LANES),
                   pl.BlockSpec((1, LANES), lambda i: (0, 0))],
        out_shape=[jax.ShapeDtypeStruct((rt, D_MODEL), _F32), jax.ShapeDtypeStruct((rt, D_MODEL), _F32),
                   jax.ShapeDtypeStruct((rt, LANES), _F32), jax.ShapeDtypeStruct((rt, LANES), _F32),
                   jax.ShapeDtypeStruct((rt, LANES), _F32), jax.ShapeDtypeStruct((1, LANES), _F32)],
        scratch_shapes=[pltpu.VMEM((1, LANES), _F32)],
        compiler_params=_cparams(("arbitrary",)),
        name="attn_out_router",
    )(o_na, o_mla, h, g_na, g_mla, w_out, g_ffn, w_r, b_r)


def _dest_kernel(idx_ref, pos_ref, pstart_ref, dest_ref, *, n_valid, trash0):
    i = pl.program_id(0)
    tb = idx_ref.shape[0]
    lane_i = lax.broadcasted_iota(jnp.int32, (tb, LANES), 1)
    lane = lane_i.astype(_F32)
    row_l = lax.broadcasted_iota(jnp.int32, (tb, 1), 0)
    valid = (row_l + i * tb) < n_valid
    idx = idx_ref[...]
    out = pos_ref[...]
    for kk in range(TOP_K):
        start = jnp.sum(jnp.where(lane == idx[:, kk:kk + 1], pstart_ref[...], 0.0), axis=-1, keepdims=True)
        out = jnp.where(lane == kk, out + start, out)
    dest_ref[...] = jnp.where(valid, out.astype(jnp.int32), trash0 + row_l * TOP_K + lane_i)


def _dest(idx, pos, pstart, n_valid, trash0):
    rt = idx.shape[0]
    tb = ROW_BLOCK
    row = pl.BlockSpec((tb, LANES), lambda i: (i, 0))
    return pl.pallas_call(
        functools.partial(_dest_kernel, n_valid=n_valid, trash0=trash0),
        grid=(rt // tb,),
        in_specs=[row, row, pl.BlockSpec((1, LANES), lambda i: (0, 0))],
        out_specs=row,
        out_shape=jax.ShapeDtypeStruct((rt, LANES), jnp.int32),
        compiler_params=_cparams(("parallel",)),
        name="moe_dest",
    )(idx, pos, pstart)


def _row_copies_wait(ref, n_rows, sem):
    pltpu.make_async_copy(ref.at[pl.ds(0, n_rows)], ref.at[pl.ds(0, n_rows)], sem).wait()


def _dispatch_kernel(dest_ref, m_ref, xs_in_ref, xs_ref, sem_rows):
    del xs_in_ref
    tb = m_ref.shape[0]

    def body(t, carry):
        for kk in range(TOP_K):
            d = dest_ref[0, 0, t * TOP_K + kk]
            pltpu.make_async_copy(m_ref.at[pl.ds(t, 1)], xs_ref.at[pl.ds(d, 1)], sem_rows).start()
        return carry

    lax.fori_loop(0, tb, body, 0)
    _row_copies_wait(xs_ref, tb * TOP_K, sem_rows)


def _dispatch(dest3, m, p_rows):
    rt = m.shape[0]
    tb = ROW_BLOCK
    xs0 = jnp.zeros((p_rows, D_MODEL), _F32)
    return pl.pallas_call(
        _dispatch_kernel,
        grid=(rt // tb,),
        in_specs=[pl.BlockSpec((1, 1, tb * TOP_K), lambda i: (i, 0, 0), memory_space=pltpu.SMEM),
                  pl.BlockSpec((tb, D_MODEL), lambda i: (i, 0)),
                  pl.BlockSpec(memory_space=pl.ANY)],
        out_specs=pl.BlockSpec(memory_space=pl.ANY),
        scratch_shapes=[pltpu.SemaphoreType.DMA(())],
        out_shape=jax.ShapeDtypeStruct((p_rows, D_MODEL), _F32),
        input_output_aliases={2: 0},
        compiler_params=_cparams(("arbitrary",)),
        name="moe_dispatch",
    )(dest3, m, xs0)


def _ffn_kernel(be_ref, nu_ref, x_ref, wu_ref, bu_ref, wd_ref, bd_ref, y_ref, wu_sc, wd_sc):
    i = pl.program_id(0)

    @pl.when(i < nu_ref[0])
    def _():
        prev = be_ref[jnp.maximum(i - 1, 0)]

        @pl.when((i == 0) | (be_ref[i] != prev))
        def _():
            wu_sc[...] = wu_ref[0].astype(_BF16)
            wd_sc[...] = wd_ref[0].astype(_BF16)

        h = _dot(x_ref[...].astype(_BF16), wu_sc[...]) + bu_ref[0]
        gate = jnp.minimum(h[:, :D_FF], SWIGLU_LIMIT)
        up = jnp.clip(h[:, D_FF:], -SWIGLU_LIMIT, SWIGLU_LIMIT)
        glu = gate * jax.nn.sigmoid(gate * SWIGLU_ALPHA)
        y_ref[...] = _dot(((up + 1.0) * glu).astype(_BF16), wd_sc[...]) + bd_ref[0]

    @pl.when(i >= nu_ref[0])
    def _():
        y_ref[...] = jnp.zeros(y_ref.shape, y_ref.dtype)


def _ffn(block_e, n_used, xs, w_up, b_up, w_down, b_down, n_blocks):
    bm = EXPERT_BLOCK
    return pl.pallas_call(
        _ffn_kernel,
        grid_spec=pltpu.PrefetchScalarGridSpec(
            num_scalar_prefetch=2,
            grid=(n_blocks,),
            in_specs=[pl.BlockSpec((bm, D_MODEL), lambda i, be, nu: (i, 0)),
                      pl.BlockSpec((1, D_MODEL, 2 * D_FF), lambda i, be, nu: (be[i], 0, 0)),
                      pl.BlockSpec((1, 1, 2 * D_FF), lambda i, be, nu: (be[i], 0, 0)),
                      pl.BlockSpec((1, D_FF, D_MODEL), lambda i, be, nu: (be[i], 0, 0)),
                      pl.BlockSpec((1, 1, D_MODEL), lambda i, be, nu: (be[i], 0, 0))],
            out_specs=pl.BlockSpec((bm, D_MODEL), lambda i, be, nu: (i, 0)),
            scratch_shapes=[pltpu.VMEM((D_MODEL, 2 * D_FF), _BF16), pltpu.VMEM((D_FF, D_MODEL), _BF16)]),
        out_shape=jax.ShapeDtypeStruct((n_blocks * bm, D_MODEL), _F32),
        compiler_params=_cparams(("arbitrary",)),
        name="moe_ffn",
    )(block_e, n_used, xs, w_up, b_up, w_down, b_down)


def _combine_kernel(dest_ref, gate_ref, h1_ref, g_ref, y_ref, o_ref, buf_sc, sem, *, final):
    tb = h1_ref.shape[0]

    def body(t, carry):
        for kk in range(TOP_K):
            d = dest_ref[0, 0, t * TOP_K + kk]
            pltpu.make_async_copy(y_ref.at[pl.ds(d, 1)], buf_sc.at[kk, pl.ds(t, 1)], sem).start()
        return carry

    lax.fori_loop(0, tb, body, 0)
    _row_copies_wait(y_ref, tb * TOP_K, sem)
    gate = gate_ref[...]
    out = h1_ref[...]
    for kk in range(TOP_K):
        out = out + gate[:, kk:kk + 1] * buf_sc[kk]
    if final:
        out = _rms(out, g_ref[...])
    o_ref[...] = out


def _combine(dest3, gate, h1, g_final, y, final):
    rt = h1.shape[0]
    tb = ROW_BLOCK
    return pl.pallas_call(
        functools.partial(_combine_kernel, final=final),
        grid=(rt // tb,),
        in_specs=[pl.BlockSpec((1, 1, tb * TOP_K), lambda i: (i, 0, 0), memory_space=pltpu.SMEM),
                  pl.BlockSpec((tb, LANES), lambda i: (i, 0)),
                  pl.BlockSpec((tb, D_MODEL), lambda i: (i, 0)),
                  pl.BlockSpec((1, D_MODEL), lambda i: (0, 0)),
                  pl.BlockSpec(memory_space=pl.ANY)],
        out_specs=pl.BlockSpec((tb, D_MODEL), lambda i: (i, 0)),
        out_shape=jax.ShapeDtypeStruct((rt, D_MODEL), _F32),
        scratch_shapes=[pltpu.VMEM((TOP_K, tb, D_MODEL), _F32), pltpu.SemaphoreType.DMA(())],
        compiler_params=_cparams(("arbitrary",)),
        name="moe_combine",
    )(dest3, gate, h1, g_final, y)


def _make_layout(seq_tokens, groups):
    n_seq = len(seq_tokens)
    tok_off = np.concatenate([[0], np.cumsum(seq_tokens)]).astype(np.int64)
    nt = int(tok_off[-1])
    n_valid = nt + n_seq * N_META
    rt = -(-n_valid // ROW_BLOCK) * ROW_BLOCK
    pos = np.zeros((rt,), np.float32)
    cidx, var, midx = [], [], []
    for s, n in enumerate(seq_tokens):
        assert n % NA_QBLOCK == 0 and n // NA_QBLOCK >= NA_KBLOCKS
        pos[tok_off[s]:tok_off[s] + n] = N_META + np.arange(n)
        pos[nt + s * N_META:nt + (s + 1) * N_META] = np.arange(N_META)
        nb = n // NA_QBLOCK
        b0 = int(tok_off[s]) // NA_QBLOCK
        for b in range(nb):
            cidx.append(b0 + min(max(b, 1), nb - 2))
            var.append(0 if b == 0 else (2 if b == nb - 1 else 1))
            midx.append(nt // N_META + s)
    assert nt % ROW_BLOCK == 0
    return {
        "seq_tokens": tuple(seq_tokens), "groups": tuple(groups), "tok_off": tuple(int(v) for v in tok_off),
        "n_tok_total": nt, "n_valid": n_valid, "rt": rt, "pos": pos,
        "na_cidx": jnp.asarray(cidx, jnp.int32), "na_var": jnp.asarray(var, jnp.int32),
        "na_midx": jnp.asarray(midx, jnp.int32),
        "seq_midx": jnp.asarray([nt // N_META + s for s in range(n_seq)], jnp.int32),
    }


def _rope_tables(pos):
    freqs = jnp.power(ROPE_THETA, -jnp.arange(0, QK_ROPE, 2, dtype=_F32) / QK_ROPE)
    ang = jnp.asarray(pos)[:, None] * freqs[None, :]
    cos, sin = jnp.cos(ang), jnp.sin(ang)
    rt = pos.shape[0]
    pad = LANES - QK_NOPE - QK_ROPE
    cos_t = jnp.concatenate([jnp.ones((rt, QK_NOPE), _F32), cos, cos, jnp.zeros((rt, pad), _F32)], axis=1)
    sin_t = jnp.concatenate([jnp.zeros((rt, QK_NOPE), _F32), sin, sin, jnp.zeros((rt, pad), _F32)], axis=1)
    return cos_t, sin_t


def _layer_weights(w_in, w_uq, w_ukv):
    half = QK_ROPE // 2
    s2 = 3 * NA_WIDTH
    kr_cols = w_in[:, s2 + Q_LORA + KV_LORA:]
    w_in_p = jnp.concatenate([w_in[:, :NA_WIDTH] * (NA_HEAD_DIM ** -0.5), w_in[:, NA_WIDTH:s2 + Q_LORA + KV_LORA],
                              kr_cols, jnp.zeros((D_MODEL, LANES - QK_ROPE), _F32)], axis=1).astype(_BF16)
    dq = QK_NOPE + QK_ROPE
    wq = w_uq.reshape(Q_LORA, MLA_HEADS, dq)
    zq = jnp.zeros((Q_LORA, MLA_HEADS, LANES - dq), _F32)
    q_plain = jnp.concatenate([wq, zq], axis=2)
    q_rot = jnp.concatenate([jnp.zeros((Q_LORA, MLA_HEADS, QK_NOPE), _F32), -wq[:, :, QK_NOPE + half:],
                             wq[:, :, QK_NOPE:QK_NOPE + half], zq], axis=2)
    w_q_p = jnp.concatenate([q_plain.reshape(Q_LORA, -1), q_rot.reshape(Q_LORA, -1)], axis=1).astype(_BF16)
    wkv = w_ukv.reshape(KV_LORA, MLA_HEADS, QK_NOPE + V_HEAD)
    k_plain = jnp.concatenate([wkv[:, :, :QK_NOPE], jnp.zeros((KV_LORA, MLA_HEADS, LANES - QK_NOPE), _F32)], axis=2)
    wv = wkv[:, :, QK_NOPE:].reshape(KV_LORA, N_PAIRS, 2, V_HEAD)
    zv = jnp.zeros((KV_LORA, N_PAIRS, LANES - V_HEAD), _F32)
    v_even = jnp.concatenate([wv[:, :, 0], zv], axis=2)
    v_odd = jnp.concatenate([zv, wv[:, :, 1]], axis=2)
    v_plain = jnp.stack([v_even, v_odd], axis=2)
    w_kv_p = jnp.concatenate([k_plain.reshape(KV_LORA, -1), v_plain.reshape(KV_LORA, -1)], axis=1).astype(_BF16)
    return w_in_p, w_q_p, w_kv_p


def _const_tables():
    half = QK_ROPE // 2
    width = MLA_HEADS * HEAD_TILE
    r_plain = np.zeros((LANES, width), np.float32)
    r_rot = np.zeros((LANES, width), np.float32)
    vone = np.zeros((1, width), np.float32)
    for h in range(MLA_HEADS):
        base = h * HEAD_TILE + QK_NOPE
        for j in range(QK_ROPE):
            r_plain[j, base + j] = 1.0
        for j in range(half):
            r_rot[half + j, base + j] = -1.0
            r_rot[j, base + half + j] = 1.0
        vone[0, h * HEAD_TILE + (V_HEAD if h % 2 == 0 else 0)] = 1.0
    w_kr_p = jnp.asarray(np.concatenate([r_plain, r_rot], axis=1), _BF16)
    return w_kr_p, jnp.asarray(vone)


def _moe_plan(counts, n_blocks):
    counts = counts[0, :N_EXPERTS].astype(jnp.int32)
    padded = (counts + EXPERT_BLOCK - 1) // EXPERT_BLOCK * EXPERT_BLOCK
    pend = jnp.cumsum(padded)
    pstart = pend - padded
    block_e = jnp.clip(jnp.searchsorted(pend, jnp.arange(n_blocks, dtype=jnp.int32) * EXPERT_BLOCK, side="right"),
                       0, N_EXPERTS - 1).astype(jnp.int32)
    n_used = (pend[-1:] // EXPERT_BLOCK).astype(jnp.int32)
    pstart_row = jnp.zeros((1, LANES), _F32).at[0, :N_EXPERTS].set(pstart.astype(_F32))
    return pstart_row, block_e, n_used


def _forward(h, layout, meta_tokens, g_attn, w_in, g_q, w_uq, g_kv, w_ukv, rpb, g_out_na, g_out_mla, w_out,
             g_ffn, w_router, b_router, w_up, b_up, w_down, b_down, g_final):
    del meta_tokens
    depth = w_in.shape[0]
    rt = layout["rt"]
    n_valid = layout["n_valid"]
    cos_t, sin_t = _rope_tables(layout["pos"])
    w_kr_p, vone = _const_tables()
    n_assign = n_valid * TOP_K
    n_blocks = -(-n_assign // EXPERT_BLOCK) + N_EXPERTS
    trash0 = n_blocks * EXPERT_BLOCK
    p_rows = trash0 + ROW_BLOCK * TOP_K
    row2 = lambda a: a.reshape(1, -1)
    for l in range(depth):
        w_in_p, w_q_p, w_kv_p = _layer_weights(w_in[l], w_uq[l], w_ukv[l])
        qna, kna, vna, qm, km, vm = _attn_in(h, cos_t, sin_t, row2(g_attn[l]), w_in_p, row2(g_q[l]), w_q_p,
                                             row2(g_kv[l]), w_kv_p, w_kr_p, vone)
        o_na = _na_attention(qna, kna, vna, _na_bias(rpb[l]), layout)
        o_mla = _mla_attention(qm, km, vm, layout)
        w_r = jnp.concatenate([w_router[l], jnp.zeros((D_MODEL, LANES - N_EXPERTS), _F32)], axis=1)
        b_r = jnp.concatenate([b_router[l], jnp.full((LANES - N_EXPERTS,), NEG_BIG, _F32)]).reshape(1, LANES)
        h1, m, idx, pos, gate, counts = _attn_out(o_na, o_mla, h, row2(g_out_na[l]), row2(g_out_mla[l]),
                                                  w_out[l].astype(_BF16), row2(g_ffn[l]), w_r, b_r, n_valid)
        pstart_row, block_e, n_used = _moe_plan(counts, n_blocks)
        dest = _dest(idx, pos, pstart_row, n_valid, trash0)
        dest3 = dest[:, :TOP_K].reshape(rt // ROW_BLOCK, 1, ROW_BLOCK * TOP_K)
        xs = _dispatch(dest3, m, p_rows)
        y = _ffn(block_e, n_used, xs, w_up[l], b_up[l].reshape(N_EXPERTS, 1, -1), w_down[l],
                 b_down[l].reshape(N_EXPERTS, 1, -1), n_blocks)
        dest3c = jnp.where(dest3 >= trash0, 0, dest3)
        h = _combine(dest3c, gate, h1, row2(g_final), y, final=(l == depth - 1))
    return h


def kernel(x_prompt, x_sample, meta_tokens, g_attn, w_in, g_q, w_uq, g_kv, w_ukv, rpb, g_out_na, g_out_mla, w_out,
           g_ffn, w_router, b_router, w_up, b_up, w_down, b_down, g_final):
    bp, lp, _ = x_prompt.shape
    bs, ls, _ = x_sample.shape
    seq_tokens = [lp] * bp + [ls] * bs
    layout = _make_layout(seq_tokens, [(0, bp, lp), (bp, bs, ls)])
    n_seq = len(seq_tokens)
    rt, n_valid = layout["rt"], layout["n_valid"]
    meta = jnp.broadcast_to(meta_tokens[None], (n_seq, N_META, D_MODEL)).reshape(n_seq * N_META, D_MODEL)
    h = jnp.concatenate([x_prompt.reshape(bp * lp, D_MODEL), x_sample.reshape(bs * ls, D_MODEL), meta,
                         jnp.zeros((rt - n_valid, D_MODEL), _F32)], axis=0)
    out = _forward(h, layout, meta_tokens, g_attn, w_in, g_q, w_uq, g_kv, w_ukv, rpb, g_out_na, g_out_mla, w_out,
                   g_ffn, w_router, b_router, w_up, b_up, w_down, b_down, g_final)
    y_prompt = out[:bp * lp].reshape(bp, lp, D_MODEL)
    y_sample = out[bp * lp:bp * lp + bs * ls].reshape(bs, ls, D_MODEL)
    return (y_prompt, y_sample)
```

```python
import functools

import numpy as np
import jax
import jax.numpy as jnp
from jax import lax
from jax.experimental import pallas as pl
from jax.experimental.pallas import tpu as pltpu

D_MODEL = 1024
GRID_W = 64
N_META = 16
NA_HEADS = 8
NA_HEAD_DIM = 64
NA_WIN_H = 8
NA_WIN_W = 16
NA_WIDTH = NA_HEADS * NA_HEAD_DIM
MLA_HEADS = 8
QK_NOPE = 64
QK_ROPE = 32
V_HEAD = 64
Q_LORA = 256
KV_LORA = 128
ROPE_THETA = 10000.0
MLA_WIDTH = MLA_HEADS * V_HEAD
N_EXPERTS = 32
TOP_K = 4
D_FF = 1024
SWIGLU_LIMIT = 7.0
SWIGLU_ALPHA = 1.702
EPS = 1e-6

LANES = 128
HEAD_TILE = LANES
N_PAIRS = NA_HEADS // 2
ROW_BLOCK = 256
NA_QROWS = 4
NA_QBLOCK = NA_QROWS * GRID_W
NA_KBLOCKS = 3
EXPERT_BLOCK = 256
MLA_TQ = 512
MLA_TK = 2048
MLA_CHUNK = 2048
NEG_BIG = -1e30
LOG2_E = 1.4426950408889634
VMEM_LIMIT = 56 * 1024 * 1024

_F32 = jnp.float32
_BF16 = jnp.bfloat16


def _cparams(sem):
    return pltpu.CompilerParams(dimension_semantics=sem, vmem_limit_bytes=VMEM_LIMIT)


def _rms(x, g):
    return x * lax.rsqrt(jnp.mean(x * x, axis=-1, keepdims=True) + EPS) * g


def _dot(a, b):
    return jnp.dot(a, b, preferred_element_type=_F32)


def _dot_nt(a, b):
    return lax.dot_general(a, b, (((1,), (1,)), ((), ())), preferred_element_type=_F32)


def _attn_in_kernel(h_ref, cos_ref, sin_ref, g_attn_ref, w_in_ref, g_q_ref, w_q_ref, g_kv_ref,
                    w_kv_ref, w_kr_ref, vone_ref,
                    qna_ref, kna_ref, vna_ref, qm_ref, km_ref, vm_ref, kmt_ref):
    a = _rms(h_ref[...], g_attn_ref[...]).astype(_BF16)
    proj = _dot(a, w_in_ref[...])
    qna_ref[...] = proj[:, 0:NA_WIDTH].astype(_BF16)
    kna_ref[...] = proj[:, NA_WIDTH:2 * NA_WIDTH].astype(_BF16)
    vna_ref[...] = proj[:, 2 * NA_WIDTH:3 * NA_WIDTH].astype(_BF16)
    s2 = 3 * NA_WIDTH
    cq = proj[:, s2:s2 + Q_LORA]
    ckv = proj[:, s2 + Q_LORA:s2 + Q_LORA + KV_LORA]
    kr = proj[:, s2 + Q_LORA + KV_LORA:]
    cos = jnp.concatenate([cos_ref[...]] * MLA_HEADS, axis=1)
    sin = jnp.concatenate([sin_ref[...]] * MLA_HEADS, axis=1)
    width = MLA_HEADS * HEAD_TILE
    q2 = _dot(_rms(cq, g_q_ref[...]).astype(_BF16), w_q_ref[...])
    scale = (QK_NOPE + QK_ROPE) ** -0.5 * LOG2_E
    qm_ref[...] = ((q2[:, :width] * cos + q2[:, width:] * sin) * scale).astype(_BF16)
    kv2 = _dot(_rms(ckv, g_kv_ref[...]).astype(_BF16), w_kv_ref[...])
    kr2 = _dot(kr.astype(_BF16), w_kr_ref[...])
    km = (kv2[:, :width] + kr2[:, :width]) * cos + kr2[:, width:] * sin
    km_ref[...] = km.astype(_BF16)
    kmt_ref[...] = km.T.astype(_BF16)
    vm_ref[...] = (kv2[:, width:] + vone_ref[...]).astype(_BF16)


def _attn_in(h, cos_t, sin_t, g_attn, w_in_p, g_q, w_q_p, g_kv, w_kv_p, w_kr_p, vone):
    rt = h.shape[0]
    tb = ROW_BLOCK
    width = MLA_HEADS * HEAD_TILE
    row = lambda w: pl.BlockSpec((tb, w), lambda i: (i, 0))
    full = lambda a: pl.BlockSpec(a.shape, lambda i: (0,) * a.ndim)
    outs = ([jax.ShapeDtypeStruct((rt, NA_WIDTH), _BF16)] * 3 + [jax.ShapeDtypeStruct((rt, width), _BF16)] * 3
            + [jax.ShapeDtypeStruct((width, rt), _BF16)])
    return pl.pallas_call(
        _attn_in_kernel,
        grid=(rt // tb,),
        in_specs=[row(D_MODEL), row(LANES), row(LANES), full(g_attn), full(w_in_p), full(g_q), full(w_q_p),
                  full(g_kv), full(w_kv_p), full(w_kr_p), full(vone)],
        out_specs=[row(NA_WIDTH)] * 3 + [row(width)] * 3 + [pl.BlockSpec((width, tb), lambda i: (0, i))],
        out_shape=outs,
        compiler_params=_cparams(("parallel",)),
        name="attn_in",
    )(h, cos_t, sin_t, g_attn, w_in_p, g_q, w_q_p, g_kv, w_kv_p, w_kr_p, vone)


def _na_kernel(cidx_ref, var_ref, midx_ref, q_ref, kp_ref, kc_ref, kn_ref, vp_ref, vc_ref, vn_ref,
               km_ref, vm_ref, bias_ref, oin_ref, o_ref):
    del cidx_ref, var_ref, midx_ref, oin_ref
    q = q_ref[...]
    lane = lax.broadcasted_iota(jnp.int32, (1, LANES), 1)
    ks = (kp_ref[...], kc_ref[...], kn_ref[...])
    vs = (vp_ref[...], vc_ref[...], vn_ref[...])
    km = km_ref[...]
    vm = vm_ref[...]
    outs = []
    for hh in range(2):
        in_head = (lane >= hh * NA_HEAD_DIM) & (lane < (hh + 1) * NA_HEAD_DIM)
        qh = jnp.where(in_head, q, jnp.zeros_like(q))
        s_loc = jnp.concatenate([_dot_nt(qh, k) for k in ks], axis=1) + bias_ref[0, hh]
        s_met = _dot_nt(qh, km)
        m = jnp.maximum(jnp.max(s_loc, axis=-1, keepdims=True), jnp.max(s_met, axis=-1, keepdims=True))
        p_loc = jnp.exp(s_loc - m)
        p_met = jnp.exp(s_met - m)
        l = jnp.sum(p_loc, axis=-1, keepdims=True) + jnp.sum(p_met, axis=-1, keepdims=True)
        o = _dot(p_met.astype(_BF16), vm)
        for j in range(NA_KBLOCKS):
            o = o + _dot(p_loc[:, j * NA_QBLOCK:(j + 1) * NA_QBLOCK].astype(_BF16), vs[j])
        outs.append(o / l)
    o_ref[...] = jnp.where(lane < NA_HEAD_DIM, outs[0], outs[1])


def _na_meta_kernel(midx_ref, q_ref, k_ref, v_ref, oin_ref, o_ref):
    del midx_ref, oin_ref
    q = q_ref[...]
    k = k_ref[...]
    v = v_ref[...]
    lane = lax.broadcasted_iota(jnp.int32, (1, LANES), 1)
    outs = []
    for hh in range(2):
        in_head = (lane >= hh * NA_HEAD_DIM) & (lane < (hh + 1) * NA_HEAD_DIM)
        s = _dot_nt(jnp.where(in_head, q, jnp.zeros_like(q)), k)
        p = jnp.exp(s - jnp.max(s, axis=-1, keepdims=True))
        outs.append(_dot(p.astype(_BF16), v) / jnp.sum(p, axis=-1, keepdims=True))
    o_ref[...] = jnp.where(lane < NA_HEAD_DIM, outs[0], outs[1])


def _na_bias(rpb_l):
    n_kr = NA_KBLOCKS * NA_QROWS
    qr = np.arange(NA_QROWS)[:, None]
    kr = np.arange(n_kr)[None, :]
    qc = np.arange(GRID_W)[:, None]
    kc = np.arange(GRID_W)[None, :]
    cs = np.clip(qc - NA_WIN_W // 2, 0, GRID_W - NA_WIN_W)
    col_ok = (kc >= cs) & (kc < cs + NA_WIN_W)
    dc = np.clip(kc - qc + NA_WIN_W - 1, 0, 2 * NA_WIN_W - 2)
    sel_c = (np.arange(2 * NA_WIN_W - 1)[None, None, :] == dc[:, :, None]) & col_ok[:, :, None]
    nrows = 4 * n_kr
    sel_r, ok = [], []
    for r0, k0 in ((0, 0), (NA_QROWS, 0), (nrows - NA_QROWS, nrows - n_kr)):
        r = r0 + qr
        key_row = k0 + kr
        rs = np.clip(r - NA_WIN_H // 2, 0, nrows - NA_WIN_H)
        row_ok = (key_row >= rs) & (key_row < rs + NA_WIN_H)
        dr = np.clip(key_row - r + NA_WIN_H - 1, 0, 2 * NA_WIN_H - 2)
        sel_r.append((np.arange(2 * NA_WIN_H - 1)[None, None, :] == dr[:, :, None]) & row_ok[:, :, None])
        ok.append(row_ok[:, None, :, None] & col_ok[None, :, None, :])
    sel_r = jnp.asarray(np.stack(sel_r), _F32)
    sel_c = jnp.asarray(sel_c, _F32)
    b = jnp.einsum("vqkd,hde,cxe->vhqckx", sel_r, rpb_l.astype(_F32), sel_c, precision=lax.Precision.HIGHEST)
    b = jnp.where(jnp.asarray(np.stack(ok))[:, None], b, NEG_BIG)
    return b.reshape(3, NA_HEADS, NA_QBLOCK, NA_KBLOCKS * NA_QBLOCK)


def _na_attention(q, k, v, bias, layout):
    rt = q.shape[0]
    cidx, var, midx = layout["na_cidx"], layout["na_var"], layout["na_midx"]
    nblk = cidx.shape[0]
    qspec = pl.BlockSpec((NA_QBLOCK, LANES), lambda p, b, c, vr, m: (b, p))
    kspec = lambda d: pl.BlockSpec((NA_QBLOCK, LANES), lambda p, b, c, vr, m: (c[b] + d, p))
    mspec = pl.BlockSpec((N_META, LANES), lambda p, b, c, vr, m: (m[b], p))
    bspec = pl.BlockSpec((1, 2, NA_QBLOCK, NA_KBLOCKS * NA_QBLOCK), lambda p, b, c, vr, m: (vr[b], p, 0, 0))
    o = pl.pallas_call(
        _na_kernel,
        grid_spec=pltpu.PrefetchScalarGridSpec(
            num_scalar_prefetch=3,
            grid=(N_PAIRS, nblk),
            in_specs=[qspec, kspec(-1), kspec(0), kspec(1), kspec(-1), kspec(0), kspec(1), mspec, mspec, bspec,
                      pl.BlockSpec(memory_space=pl.ANY)],
            out_specs=qspec),
        out_shape=jax.ShapeDtypeStruct((rt, NA_WIDTH), _F32),
        input_output_aliases={13: 0},
        compiler_params=_cparams(("parallel", "parallel")),
        name="na_attn",
    )(cidx, var, midx, q, k, k, k, v, v, v, k, v, bias, jnp.zeros((rt, NA_WIDTH), _F32))
    smidx = layout["seq_midx"]
    mq = pl.BlockSpec((N_META, LANES), lambda p, s, m: (m[s], p))
    return pl.pallas_call(
        _na_meta_kernel,
        grid_spec=pltpu.PrefetchScalarGridSpec(
            num_scalar_prefetch=1,
            grid=(N_PAIRS, smidx.shape[0]),
            in_specs=[mq, mq, mq, pl.BlockSpec(memory_space=pl.ANY)],
            out_specs=mq),
        out_shape=jax.ShapeDtypeStruct((rt, NA_WIDTH), _F32),
        input_output_aliases={4: 0},
        compiler_params=_cparams(("parallel", "parallel")),
        name="na_meta",
    )(smidx, q, k, v, o)


def _online_update(m_prev, acc, s, v):
    w = s.shape[1]
    if w % LANES == 0:
        parts = [s[:, j * LANES:(j + 1) * LANES] for j in range(w // LANES)]
        mx = parts[0]
        for part in parts[1:]:
            mx = jnp.maximum(mx, part)
        m_new = jnp.maximum(m_prev, jnp.max(mx, axis=-1, keepdims=True))
        p = jnp.concatenate([jnp.exp2(part - m_new) for part in parts], axis=1)
    else:
        m_new = jnp.maximum(m_prev, jnp.max(s, axis=-1, keepdims=True))
        p = jnp.exp2(s - m_new[:, :w])
    acc = jnp.exp2(m_prev - m_new) * acc + _dot(p.astype(_BF16), v)
    return m_new, acc


def _mla_kernel(q_ref, kt_ref, v_ref, km_ref, vm_ref, oin_ref, o_ref, m_sc, acc_sc):
    del oin_ref
    t = pl.program_id(3)
    nt = pl.num_programs(3)
    tk = v_ref.shape[0]
    chunk = min(MLA_CHUNK, tk)

    @pl.when(t == 0)
    def _():
        m_sc[...] = jnp.full(m_sc.shape, NEG_BIG, _F32)
        acc_sc[...] = jnp.zeros(acc_sc.shape, _F32)

    for hh in range(2):
        tile = slice(hh * HEAD_TILE, (hh + 1) * HEAD_TILE)
        q = q_ref[:, tile]
        m, acc = m_sc[hh], acc_sc[hh]
        for c in range(tk // chunk):
            cols = slice(c * chunk, (c + 1) * chunk)
            m, acc = _online_update(m, acc, _dot(q, kt_ref[tile, cols]), v_ref[cols, tile])
        m_sc[hh] = m
        acc_sc[hh] = acc

    @pl.when(t == nt - 1)
    def _():
        outs = []
        for hh in range(2):
            tile = slice(hh * HEAD_TILE, (hh + 1) * HEAD_TILE)
            _, acc = _online_update(m_sc[hh], acc_sc[hh], _dot_nt(q_ref[:, tile], km_ref[:, tile]), vm_ref[:, tile])
            outs.append(acc)
        lane = lax.broadcasted_iota(jnp.int32, (1, LANES), 1)
        l0 = outs[0][:, V_HEAD:V_HEAD + 1]
        l1 = outs[1][:, 0:1]
        o_ref[...] = jnp.where(lane < V_HEAD, outs[0] / l0, outs[1] / l1)


def _mla_call(q, kt, k, v, o_prev, *, tq, tk, n_seq, q_blk0, q_blk_stride, n_qblk, kv_blk0, kv_blk_stride, n_kvblk,
              meta_blk0, name):
    rt = q.shape[0]
    pw = 2 * HEAD_TILE
    qspec = pl.BlockSpec((tq, pw), lambda s, p, i, t: (q_blk0 + s * q_blk_stride + i, p))
    ktspec = pl.BlockSpec((pw, tk), lambda s, p, i, t: (p, kv_blk0 + s * kv_blk_stride + t))
    vspec = pl.BlockSpec((tk, pw), lambda s, p, i, t: (kv_blk0 + s * kv_blk_stride + t, p))
    mspec = pl.BlockSpec((N_META, pw), lambda s, p, i, t: (meta_blk0 + s, p))
    ospec = pl.BlockSpec((tq, LANES), lambda s, p, i, t: (q_blk0 + s * q_blk_stride + i, p))
    return pl.pallas_call(
        _mla_kernel,
        grid=(n_seq, N_PAIRS, n_qblk, n_kvblk),
        in_specs=[qspec, ktspec, vspec, mspec, mspec, pl.BlockSpec(memory_space=pl.ANY)],
        out_specs=ospec,
        out_shape=jax.ShapeDtypeStruct((rt, MLA_WIDTH), _F32),
        scratch_shapes=[pltpu.VMEM((2, tq, LANES), _F32), pltpu.VMEM((2, tq, LANES), _F32)],
        input_output_aliases={5: 0},
        compiler_params=_cparams(("parallel", "parallel", "parallel", "arbitrary")),
        name=name,
    )(q, kt, v, k, v, o_prev)


def _mla_attention(q, kt, k, v, layout):
    rt = q.shape[0]
    o = jnp.zeros((rt, MLA_WIDTH), _F32)
    nt = layout["n_tok_total"]
    for gi, (seq0, n_seq, n_tok) in enumerate(layout["groups"]):
        tq = min(MLA_TQ, n_tok)
        tk = min(MLA_TK, n_tok)
        off = layout["tok_off"][seq0]
        o = _mla_call(q, kt, k, v, o, tq=tq, tk=tk, n_seq=n_seq, q_blk0=off // tq, q_blk_stride=n_tok // tq,
                      n_qblk=n_tok // tq, kv_blk0=off // tk, kv_blk_stride=n_tok // tk, n_kvblk=n_tok // tk,
                      meta_blk0=nt // N_META + seq0, name=f"mla_tok{gi}")
        o = _mla_call(q, kt, k, v, o, tq=N_META, tk=tk, n_seq=n_seq, q_blk0=nt // N_META + seq0, q_blk_stride=1,
                      n_qblk=1, kv_blk0=off // tk, kv_blk_stride=n_tok // tk, n_kvblk=n_tok // tk,
                      meta_blk0=nt // N_META + seq0, name=f"mla_meta{gi}")
    return o


def _attn_out_kernel(ona_ref, omla_ref, h_ref, g_na_ref, g_mla_ref, w_out_ref, g_ffn_ref, w_r_ref, b_r_ref,
                     h1_ref, m_ref, idx_ref, pos_ref, gate_ref, cnt_ref, cnt_sc, *, n_valid):
    i = pl.program_id(0)
    tb = h_ref.shape[0]

    @pl.when(i == 0)
    def _():
        cnt_sc[...] = jnp.zeros(cnt_sc.shape, _F32)

    n1 = _rms(ona_ref[...], g_na_ref[...]).astype(_BF16)
    n2 = _rms(omla_ref[...], g_mla_ref[...]).astype(_BF16)
    h1 = h_ref[...] + _dot(n1, w_out_ref[0:NA_WIDTH, :]) + _dot(n2, w_out_ref[NA_WIDTH:, :])
    h1_ref[...] = h1
    m = _rms(h1, g_ffn_ref[...])
    m_ref[...] = m
    logits = jnp.dot(m, w_r_ref[...], preferred_element_type=_F32, precision=lax.Precision.HIGHEST) + b_r_ref[...]

    lane = lax.broadcasted_iota(jnp.int32, (tb, LANES), 1).astype(_F32)
    row = lax.broadcasted_iota(jnp.int32, (tb, 1), 0) + i * tb
    valid = jnp.where(row < n_valid, 1.0, 0.0)
    work = logits
    sel = jnp.zeros((tb, LANES), _F32)
    idx_out = jnp.zeros((tb, LANES), _F32)
    top = []
    for kk in range(TOP_K):
        mx = jnp.max(work, axis=-1, keepdims=True)
        idx = jnp.min(jnp.where(work == mx, lane, float(LANES)), axis=-1, keepdims=True)
        hit = lane == idx
        sel = jnp.where(hit, 1.0, sel)
        work = jnp.where(hit, NEG_BIG * 2, work)
        idx_out = jnp.where(lane == kk, idx, idx_out)
        top.append((mx, idx))
    e = [jnp.exp(mx - top[0][0]) for mx, _ in top]
    denom = e[0] + e[1] + e[2] + e[3]
    gate_out = jnp.zeros((tb, LANES), _F32)
    for kk in range(TOP_K):
        gate_out = jnp.where(lane == kk, e[kk] / denom, gate_out)
    gate_ref[...] = gate_out * valid
    idx_ref[...] = idx_out

    sel = sel * valid
    r_i = lax.broadcasted_iota(jnp.int32, (tb, tb), 0)
    c_i = lax.broadcasted_iota(jnp.int32, (tb, tb), 1)
    tri = jnp.where(c_i < r_i, 1.0, 0.0).astype(_BF16)
    pos_full = _dot(tri, sel.astype(_BF16)) + cnt_sc[...]
    pos_out = jnp.zeros((tb, LANES), _F32)
    for kk in range(TOP_K):
        pk = jnp.sum(jnp.where(lane == top[kk][1], pos_full, 0.0), axis=-1, keepdims=True)
        pos_out = jnp.where(lane == kk, pk, pos_out)
    pos_ref[...] = pos_out
    cnt_sc[...] = cnt_sc[...] + jnp.sum(sel, axis=0, keepdims=True)
    cnt_ref[...] = cnt_sc[...]


def _attn_out(o_na, o_mla, h, g_na, g_mla, w_out, g_ffn, w_r, b_r, n_valid):
    rt = h.shape[0]
    tb = ROW_BLOCK
    row = lambda w: pl.BlockSpec((tb, w), lambda i: (i, 0))
    full = lambda a: pl.BlockSpec(a.shape, lambda i: (0,) * a.ndim)
    return pl.pallas_call(
        functools.partial(_attn_out_kernel, n_valid=n_valid),
        grid=(rt // tb,),
        in_specs=[row(NA_WIDTH), row(MLA_WIDTH), row(D_MODEL), full(g_na), full(g_mla), full(w_out), full(g_ffn),
                  full(w_r), full(b_r)],
        out_specs=[row(D_MODEL), row(D_MODEL), row(LANES), row(LANES), row(LANES),
                   pl.BlockSpec((1, LANES), lambda i: (0, 0))],
        out_shape=[jax.ShapeDtypeStruct((rt, D_MODEL), _F32), jax.ShapeDtypeStruct((rt, D_MODEL), _F32),
                   jax.ShapeDtypeStruct((rt, LANES), _F32), jax.ShapeDtypeStruct((rt, LANES), _F32),
                   jax.ShapeDtypeStruct((rt, LANES), _F32), jax.ShapeDtypeStruct((1, LANES), _F32)],
        scratch_shapes=[pltpu.VMEM((1, LANES), _F32)],
        compiler_params=_cparams(("arbitrary",)),
        name="attn_out_router",
    )(o_na, o_mla, h, g_na, g_mla, w_out, g_ffn, w_r, b_r)


def _dest_kernel(idx_ref, pos_ref, pstart_ref, dest_ref, *, n_valid, trash0):
    i = pl.program_id(0)
    tb = idx_ref.shape[0]
    lane_i = lax.broadcasted_iota(jnp.int32, (tb, LANES), 1)
    lane = lane_i.astype(_F32)
    row = lax.broadcasted_iota(jnp.int32, (tb, 1), 0) + i * tb
    valid = row < n_valid
    idx = idx_ref[...]
    out = pos_ref[...]
    for kk in range(TOP_K):
        start = jnp.sum(jnp.where(lane == idx[:, kk:kk + 1], pstart_ref[...], 0.0), axis=-1, keepdims=True)
        out = jnp.where(lane == kk, out + start, out)
    dest_ref[...] = jnp.where(valid, out.astype(jnp.int32), trash0 + (row - n_valid) * TOP_K + lane_i)


def _dest(idx, pos, pstart, n_valid, trash0):
    rt = idx.shape[0]
    tb = ROW_BLOCK
    row = pl.BlockSpec((tb, LANES), lambda i: (i, 0))
    return pl.pallas_call(
        functools.partial(_dest_kernel, n_valid=n_valid, trash0=trash0),
        grid=(rt // tb,),
        in_specs=[row, row, pl.BlockSpec((1, LANES), lambda i: (0, 0))],
        out_specs=row,
        out_shape=jax.ShapeDtypeStruct((rt, LANES), jnp.int32),
        compiler_params=_cparams(("parallel",)),
        name="moe_dest",
    )(idx, pos, pstart)


def _row_copies_wait(ref, n_rows, sem):
    pltpu.make_async_copy(ref.at[pl.ds(0, n_rows)], ref.at[pl.ds(0, n_rows)], sem).wait()


def _dispatch_kernel(free_ref, dest_ref, m_ref, xs_ref, zero_sc, sem_free, sem_rows):
    i = pl.program_id(0)
    tb = m_ref.shape[0]
    n_free = free_ref.shape[0]

    @pl.when(i == 0)
    def _():
        zero_sc[...] = jnp.zeros(zero_sc.shape, zero_sc.dtype)

        def fill(j, carry):
            pltpu.make_async_copy(zero_sc.at[pl.ds(0, 1)], xs_ref.at[pl.ds(free_ref[j], 1)], sem_free).start()
            return carry

        lax.fori_loop(0, n_free, fill, 0)
        _row_copies_wait(xs_ref, n_free, sem_free)

    def body(t, carry):
        for kk in range(TOP_K):
            d = dest_ref[0, 0, t * TOP_K + kk]
            pltpu.make_async_copy(m_ref.at[pl.ds(t, 1)], xs_ref.at[pl.ds(d, 1)], sem_rows).start()
        return carry

    lax.fori_loop(0, tb, body, 0)
    _row_copies_wait(xs_ref, tb * TOP_K, sem_rows)


def _dispatch(free_slots, dest3, m, p_rows):
    rt = m.shape[0]
    tb = ROW_BLOCK
    return pl.pallas_call(
        _dispatch_kernel,
        grid_spec=pltpu.PrefetchScalarGridSpec(
            num_scalar_prefetch=1,
            grid=(rt // tb,),
            in_specs=[pl.BlockSpec((1, 1, tb * TOP_K), lambda i, f: (i, 0, 0), memory_space=pltpu.SMEM),
                      pl.BlockSpec((tb, D_MODEL), lambda i, f: (i, 0))],
            out_specs=pl.BlockSpec(memory_space=pl.ANY),
            scratch_shapes=[pltpu.VMEM((8, D_MODEL), _F32), pltpu.SemaphoreType.DMA(()),
                            pltpu.SemaphoreType.DMA(())]),
        out_shape=jax.ShapeDtypeStruct((p_rows, D_MODEL), _F32),
        compiler_params=_cparams(("arbitrary",)),
        name="moe_dispatch",
    )(free_slots, dest3, m)


def _ffn_kernel(be_ref, nu_ref, x_ref, wu_ref, bu_ref, wd_ref, bd_ref, y_ref, wu_sc, wd_sc):
    i = pl.program_id(0)

    @pl.when(i < nu_ref[0])
    def _():
        prev = be_ref[jnp.maximum(i - 1, 0)]

        @pl.when((i == 0) | (be_ref[i] != prev))
        def _():
            wu_sc[...] = wu_ref[0].astype(_BF16)
            wd_sc[...] = wd_ref[0].astype(_BF16)

        h = _dot(x_ref[...].astype(_BF16), wu_sc[...]) + bu_ref[0]
        gate = jnp.minimum(h[:, :D_FF], SWIGLU_LIMIT)
        up = jnp.clip(h[:, D_FF:], -SWIGLU_LIMIT, SWIGLU_LIMIT)
        glu = gate * jax.nn.sigmoid(gate * SWIGLU_ALPHA)
        y_ref[...] = _dot(((up + 1.0) * glu).astype(_BF16), wd_sc[...]) + bd_ref[0]

    @pl.when(i >= nu_ref[0])
    def _():
        y_ref[...] = jnp.zeros(y_ref.shape, y_ref.dtype)


def _ffn(block_e, n_used, xs, w_up, b_up, w_down, b_down, n_blocks):
    bm = EXPERT_BLOCK
    return pl.pallas_call(
        _ffn_kernel,
        grid_spec=pltpu.PrefetchScalarGridSpec(
            num_scalar_prefetch=2,
            grid=(n_blocks,),
            in_specs=[pl.BlockSpec((bm, D_MODEL), lambda i, be, nu: (i, 0)),
                      pl.BlockSpec((1, D_MODEL, 2 * D_FF), lambda i, be, nu: (be[i], 0, 0)),
                      pl.BlockSpec((1, 1, 2 * D_FF), lambda i, be, nu: (be[i], 0, 0)),
                      pl.BlockSpec((1, D_FF, D_MODEL), lambda i, be, nu: (be[i], 0, 0)),
                      pl.BlockSpec((1, 1, D_MODEL), lambda i, be, nu: (be[i], 0, 0))],
            out_specs=pl.BlockSpec((bm, D_MODEL), lambda i, be, nu: (i, 0)),
            scratch_shapes=[pltpu.VMEM((D_MODEL, 2 * D_FF), _BF16), pltpu.VMEM((D_FF, D_MODEL), _BF16)]),
        out_shape=jax.ShapeDtypeStruct((n_blocks * bm, D_MODEL), _F32),
        compiler_params=_cparams(("arbitrary",)),
        name="moe_ffn",
    )(block_e, n_used, xs, w_up, b_up, w_down, b_down)


def _combine_kernel(dest_ref, gate_ref, h1_ref, g_ref, y_ref, o_ref, buf_sc, sem, *, final):
    tb = h1_ref.shape[0]

    def body(t, carry):
        for kk in range(TOP_K):
            d = dest_ref[0, 0, t * TOP_K + kk]
            pltpu.make_async_copy(y_ref.at[pl.ds(d, 1)], buf_sc.at[kk, pl.ds(t, 1)], sem).start()
        return carry

    lax.fori_loop(0, tb, body, 0)
    _row_copies_wait(y_ref, tb * TOP_K, sem)
    gate = gate_ref[...]
    out = h1_ref[...]
    for kk in range(TOP_K):
        out = out + gate[:, kk:kk + 1] * buf_sc[kk]
    if final:
        out = _rms(out, g_ref[...])
    o_ref[...] = out


def _combine(dest3, gate, h1, g_final, y, final, blk0, nblk):
    tb = ROW_BLOCK
    return pl.pallas_call(
        functools.partial(_combine_kernel, final=final),
        grid=(nblk,),
        in_specs=[pl.BlockSpec((1, 1, tb * TOP_K), lambda i: (blk0 + i, 0, 0), memory_space=pltpu.SMEM),
                  pl.BlockSpec((tb, LANES), lambda i: (blk0 + i, 0)),
                  pl.BlockSpec((tb, D_MODEL), lambda i: (blk0 + i, 0)),
                  pl.BlockSpec((1, D_MODEL), lambda i: (0, 0)),
                  pl.BlockSpec(memory_space=pl.ANY)],
        out_specs=pl.BlockSpec((tb, D_MODEL), lambda i: (i, 0)),
        out_shape=jax.ShapeDtypeStruct((nblk * tb, D_MODEL), _F32),
        scratch_shapes=[pltpu.VMEM((TOP_K, tb, D_MODEL), _F32), pltpu.SemaphoreType.DMA(())],
        compiler_params=_cparams(("arbitrary",)),
        name="moe_combine",
    )(dest3, gate, h1, g_final, y)


def _make_layout(seq_tokens, groups):
    n_seq = len(seq_tokens)
    tok_off = np.concatenate([[0], np.cumsum(seq_tokens)]).astype(np.int64)
    nt = int(tok_off[-1])
    n_valid = nt + n_seq * N_META
    rt = -(-n_valid // ROW_BLOCK) * ROW_BLOCK
    pos = np.zeros((rt,), np.float32)
    cidx, var, midx = [], [], []
    for s, n in enumerate(seq_tokens):
        assert n % NA_QBLOCK == 0 and n // NA_QBLOCK >= NA_KBLOCKS
        pos[tok_off[s]:tok_off[s] + n] = N_META + np.arange(n)
        pos[nt + s * N_META:nt + (s + 1) * N_META] = np.arange(N_META)
        nb = n // NA_QBLOCK
        b0 = int(tok_off[s]) // NA_QBLOCK
        for b in range(nb):
            cidx.append(b0 + min(max(b, 1), nb - 2))
            var.append(0 if b == 0 else (2 if b == nb - 1 else 1))
            midx.append(nt // N_META + s)
    assert nt % ROW_BLOCK == 0
    return {
        "seq_tokens": tuple(seq_tokens), "groups": tuple(groups), "tok_off": tuple(int(v) for v in tok_off),
        "n_tok_total": nt, "n_valid": n_valid, "rt": rt, "pos": pos,
        "na_cidx": jnp.asarray(cidx, jnp.int32), "na_var": jnp.asarray(var, jnp.int32),
        "na_midx": jnp.asarray(midx, jnp.int32),
        "seq_midx": jnp.asarray([nt // N_META + s for s in range(n_seq)], jnp.int32),
    }


def _rope_tables(pos):
    freqs = jnp.power(ROPE_THETA, -jnp.arange(0, QK_ROPE, 2, dtype=_F32) / QK_ROPE)
    ang = jnp.asarray(pos)[:, None] * freqs[None, :]
    cos, sin = jnp.cos(ang), jnp.sin(ang)
    rt = pos.shape[0]
    pad = LANES - QK_NOPE - QK_ROPE
    cos_t = jnp.concatenate([jnp.ones((rt, QK_NOPE), _F32), cos, cos, jnp.zeros((rt, pad), _F32)], axis=1)
    sin_t = jnp.concatenate([jnp.zeros((rt, QK_NOPE), _F32), sin, sin, jnp.zeros((rt, pad), _F32)], axis=1)
    return cos_t, sin_t


def _layer_weights(w_in, w_uq, w_ukv):
    half = QK_ROPE // 2
    s2 = 3 * NA_WIDTH
    kr_cols = w_in[:, s2 + Q_LORA + KV_LORA:]
    w_in_p = jnp.concatenate([w_in[:, :NA_WIDTH] * (NA_HEAD_DIM ** -0.5), w_in[:, NA_WIDTH:s2 + Q_LORA + KV_LORA],
                              kr_cols, jnp.zeros((D_MODEL, LANES - QK_ROPE), _F32)], axis=1).astype(_BF16)
    dq = QK_NOPE + QK_ROPE
    wq = w_uq.reshape(Q_LORA, MLA_HEADS, dq)
    zq = jnp.zeros((Q_LORA, MLA_HEADS, LANES - dq), _F32)
    q_plain = jnp.concatenate([wq, zq], axis=2)
    q_rot = jnp.concatenate([jnp.zeros((Q_LORA, MLA_HEADS, QK_NOPE), _F32), -wq[:, :, QK_NOPE + half:],
                             wq[:, :, QK_NOPE:QK_NOPE + half], zq], axis=2)
    w_q_p = jnp.concatenate([q_plain.reshape(Q_LORA, -1), q_rot.reshape(Q_LORA, -1)], axis=1).astype(_BF16)
    wkv = w_ukv.reshape(KV_LORA, MLA_HEADS, QK_NOPE + V_HEAD)
    k_plain = jnp.concatenate([wkv[:, :, :QK_NOPE], jnp.zeros((KV_LORA, MLA_HEADS, LANES - QK_NOPE), _F32)], axis=2)
    wv = wkv[:, :, QK_NOPE:].reshape(KV_LORA, N_PAIRS, 2, V_HEAD)
    zv = jnp.zeros((KV_LORA, N_PAIRS, LANES - V_HEAD), _F32)
    v_even = jnp.concatenate([wv[:, :, 0], zv], axis=2)
    v_odd = jnp.concatenate([zv, wv[:, :, 1]], axis=2)
    v_plain = jnp.stack([v_even, v_odd], axis=2)
    w_kv_p = jnp.concatenate([k_plain.reshape(KV_LORA, -1), v_plain.reshape(KV_LORA, -1)], axis=1).astype(_BF16)
    return w_in_p, w_q_p, w_kv_p


def _const_tables():
    half = QK_ROPE // 2
    width = MLA_HEADS * HEAD_TILE
    r_plain = np.zeros((LANES, width), np.float32)
    r_rot = np.zeros((LANES, width), np.float32)
    vone = np.zeros((1, width), np.float32)
    for h in range(MLA_HEADS):
        base = h * HEAD_TILE + QK_NOPE
        for j in range(QK_ROPE):
            r_plain[j, base + j] = 1.0
        for j in range(half):
            r_rot[half + j, base + j] = -1.0
            r_rot[j, base + half + j] = 1.0
        vone[0, h * HEAD_TILE + (V_HEAD if h % 2 == 0 else 0)] = 1.0
    w_kr_p = jnp.asarray(np.concatenate([r_plain, r_rot], axis=1), _BF16)
    return w_kr_p, jnp.asarray(vone)


def _moe_plan(counts, n_blocks, n_free):
    counts = counts[0, :N_EXPERTS].astype(jnp.int32)
    padded = (counts + EXPERT_BLOCK - 1) // EXPERT_BLOCK * EXPERT_BLOCK
    pend = jnp.cumsum(padded)
    pstart = pend - padded
    first_row = jnp.arange(n_blocks, dtype=jnp.int32) * EXPERT_BLOCK
    block_e = jnp.minimum(jnp.sum((pend[None, :] <= first_row[:, None]).astype(jnp.int32), axis=1), N_EXPERTS - 1)
    n_used = (pend[-1:] // EXPERT_BLOCK).astype(jnp.int32)
    pstart_row = jnp.zeros((1, LANES), _F32).at[0, :N_EXPERTS].set(pstart.astype(_F32))
    gap = jnp.concatenate([padded - counts, (n_blocks * EXPERT_BLOCK - pend[-1:])])
    gap_end = jnp.cumsum(gap)
    gap_first = jnp.concatenate([pstart + counts, pend[-1:]])
    j = jnp.arange(n_free, dtype=jnp.int32)
    seg = jnp.sum((gap_end[None, :] <= j[:, None]).astype(jnp.int32), axis=1)
    free_slots = (gap_first[seg] + j - (gap_end - gap)[seg]).astype(jnp.int32)
    return pstart_row, block_e, n_used, free_slots


def _forward(h, layout, meta_tokens, g_attn, w_in, g_q, w_uq, g_kv, w_ukv, rpb, g_out_na, g_out_mla, w_out,
             g_ffn, w_router, b_router, w_up, b_up, w_down, b_down, g_final):
    del meta_tokens
    depth = w_in.shape[0]
    rt = layout["rt"]
    n_valid = layout["n_valid"]
    cos_t, sin_t = _rope_tables(layout["pos"])
    w_kr_p, vone = _const_tables()
    n_assign = n_valid * TOP_K
    n_blocks = -(-n_assign // EXPERT_BLOCK) + N_EXPERTS
    trash0 = n_blocks * EXPERT_BLOCK
    p_rows = trash0 + (rt - n_valid) * TOP_K
    n_free = trash0 - n_assign
    row2 = lambda a: a.reshape(1, -1)
    for l in range(depth):
        w_in_p, w_q_p, w_kv_p = _layer_weights(w_in[l], w_uq[l], w_ukv[l])
        qna, kna, vna, qm, km, vm, kmt = _attn_in(h, cos_t, sin_t, row2(g_attn[l]), w_in_p, row2(g_q[l]), w_q_p,
                                             row2(g_kv[l]), w_kv_p, w_kr_p, vone)
        o_na = _na_attention(qna, kna, vna, _na_bias(rpb[l]), layout)
        o_mla = _mla_attention(qm, kmt, km, vm, layout)
        w_r = jnp.concatenate([w_router[l], jnp.zeros((D_MODEL, LANES - N_EXPERTS), _F32)], axis=1)
        b_r = jnp.concatenate([b_router[l], jnp.full((LANES - N_EXPERTS,), NEG_BIG, _F32)]).reshape(1, LANES)
        h1, m, idx, pos, gate, counts = _attn_out(o_na, o_mla, h, row2(g_out_na[l]), row2(g_out_mla[l]),
                                                  w_out[l].astype(_BF16), row2(g_ffn[l]), w_r, b_r, n_valid)
        pstart_row, block_e, n_used, free_slots = _moe_plan(counts, n_blocks, n_free)
        dest = _dest(idx, pos, pstart_row, n_valid, trash0)
        dest3 = dest[:, :TOP_K].reshape(rt // ROW_BLOCK, 1, ROW_BLOCK * TOP_K)
        xs = _dispatch(free_slots, dest3, m, p_rows)
        y = _ffn(block_e, n_used, xs, w_up[l], b_up[l].reshape(N_EXPERTS, 1, -1), w_down[l],
                 b_down[l].reshape(N_EXPERTS, 1, -1), n_blocks)
        dest3c = jnp.where(dest3 >= trash0, 0, dest3)
        if l < depth - 1:
            h = _combine(dest3c, gate, h1, row2(g_final), y, False, 0, rt // ROW_BLOCK)
    outs = []
    for seq0, n_seq, n_tok in layout["groups"]:
        outs.append(_combine(dest3c, gate, h1, row2(g_final), y, True, layout["tok_off"][seq0] // ROW_BLOCK,
                             n_seq * n_tok // ROW_BLOCK))
    return outs


def kernel(x_prompt, x_sample, meta_tokens, g_attn, w_in, g_q, w_uq, g_kv, w_ukv, rpb, g_out_na, g_out_mla, w_out,
           g_ffn, w_router, b_router, w_up, b_up, w_down, b_down, g_final):
    bp, lp, _ = x_prompt.shape
    bs, ls, _ = x_sample.shape
    seq_tokens = [lp] * bp + [ls] * bs
    layout = _make_layout(seq_tokens, [(0, bp, lp), (bp, bs, ls)])
    n_seq = len(seq_tokens)
    rt, n_valid = layout["rt"], layout["n_valid"]
    meta = jnp.broadcast_to(meta_tokens[None], (n_seq, N_META, D_MODEL)).reshape(n_seq * N_META, D_MODEL)
    h = jnp.concatenate([x_prompt.reshape(bp * lp, D_MODEL), x_sample.reshape(bs * ls, D_MODEL), meta,
                         jnp.zeros((rt - n_valid, D_MODEL), _F32)], axis=0)
    out = _forward(h, layout, meta_tokens, g_attn, w_in, g_q, w_uq, g_kv, w_ukv, rpb, g_out_na, g_out_mla, w_out,
                   g_ffn, w_router, b_router, w_up, b_up, w_down, b_down, g_final)
    return (out[0].reshape(bp, lp, D_MODEL), out[1].reshape(bs, ls, D_MODEL))
```

```python
import functools

import numpy as np
import jax
import jax.numpy as jnp
from jax import lax
from jax.experimental import pallas as pl
from jax.experimental.pallas import tpu as pltpu

D_MODEL = 1024
GRID_W = 64
N_META = 16
NA_HEADS = 8
NA_HEAD_DIM = 64
NA_WIN_H = 8
NA_WIN_W = 16
NA_WIDTH = NA_HEADS * NA_HEAD_DIM
MLA_HEADS = 8
QK_NOPE = 64
QK_ROPE = 32
V_HEAD = 64
Q_LORA = 256
KV_LORA = 128
ROPE_THETA = 10000.0
MLA_WIDTH = MLA_HEADS * V_HEAD
N_EXPERTS = 32
TOP_K = 4
D_FF = 1024
SWIGLU_LIMIT = 7.0
SWIGLU_ALPHA = 1.702
EPS = 1e-6

LANES = 128
HEAD_TILE = LANES
N_PAIRS = NA_HEADS // 2
ROW_BLOCK = 256
NA_QROWS = 4
NA_QBLOCK = NA_QROWS * GRID_W
NA_KBLOCKS = 3
EXPERT_BLOCK = 256
ROW_TILE = (D_MODEL // LANES, LANES)
MLA_TQ = 512
MLA_TK = 2048
MLA_CHUNK = 2048
NEG_BIG = -1e30
LOG2_E = 1.4426950408889634
VMEM_LIMIT = 56 * 1024 * 1024

_F32 = jnp.float32
_BF16 = jnp.bfloat16


def _cparams(sem):
    return pltpu.CompilerParams(dimension_semantics=sem, vmem_limit_bytes=VMEM_LIMIT)


def _rms(x, g):
    return x * lax.rsqrt(jnp.mean(x * x, axis=-1, keepdims=True) + EPS) * g


def _dot(a, b):
    return jnp.dot(a, b, preferred_element_type=_F32)


def _dot_nt(a, b):
    return lax.dot_general(a, b, (((1,), (1,)), ((), ())), preferred_element_type=_F32)


def _attn_in_kernel(h_ref, cos_ref, sin_ref, g_attn_ref, w_in_ref, g_q_ref, w_q_ref, g_kv_ref,
                    w_kv_ref, w_kr_ref, vone_ref,
                    qna_ref, kna_ref, vna_ref, qm_ref, km_ref, vm_ref, kmt_ref):
    a = _rms(h_ref[...], g_attn_ref[...]).astype(_BF16)
    proj = _dot(a, w_in_ref[...])
    qna_ref[...] = proj[:, 0:NA_WIDTH].astype(_BF16)
    kna_ref[...] = proj[:, NA_WIDTH:2 * NA_WIDTH].astype(_BF16)
    vna_ref[...] = proj[:, 2 * NA_WIDTH:3 * NA_WIDTH].astype(_BF16)
    s2 = 3 * NA_WIDTH
    cq = proj[:, s2:s2 + Q_LORA]
    ckv = proj[:, s2 + Q_LORA:s2 + Q_LORA + KV_LORA]
    kr = proj[:, s2 + Q_LORA + KV_LORA:]
    cos = jnp.concatenate([cos_ref[...]] * MLA_HEADS, axis=1)
    sin = jnp.concatenate([sin_ref[...]] * MLA_HEADS, axis=1)
    width = MLA_HEADS * HEAD_TILE
    q2 = _dot(_rms(cq, g_q_ref[...]).astype(_BF16), w_q_ref[...])
    scale = (QK_NOPE + QK_ROPE) ** -0.5 * LOG2_E
    qm_ref[...] = ((q2[:, :width] * cos + q2[:, width:] * sin) * scale).astype(_BF16)
    kv2 = _dot(_rms(ckv, g_kv_ref[...]).astype(_BF16), w_kv_ref[...])
    kr2 = _dot(kr.astype(_BF16), w_kr_ref[...])
    km = (kv2[:, :width] + kr2[:, :width]) * cos + kr2[:, width:] * sin
    km_ref[...] = km.astype(_BF16)
    kmt_ref[...] = km.T.astype(_BF16)
    vm_ref[...] = (kv2[:, width:] + vone_ref[...]).astype(_BF16)


def _attn_in(h, cos_t, sin_t, g_attn, w_in_p, g_q, w_q_p, g_kv, w_kv_p, w_kr_p, vone):
    rt = h.shape[0]
    tb = ROW_BLOCK
    width = MLA_HEADS * HEAD_TILE
    row = lambda w: pl.BlockSpec((tb, w), lambda i: (i, 0))
    full = lambda a: pl.BlockSpec(a.shape, lambda i: (0,) * a.ndim)
    outs = ([jax.ShapeDtypeStruct((rt, NA_WIDTH), _BF16)] * 3 + [jax.ShapeDtypeStruct((rt, width), _BF16)] * 3
            + [jax.ShapeDtypeStruct((width, rt), _BF16)])
    return pl.pallas_call(
        _attn_in_kernel,
        grid=(rt // tb,),
        in_specs=[row(D_MODEL), row(LANES), row(LANES), full(g_attn), full(w_in_p), full(g_q), full(w_q_p),
                  full(g_kv), full(w_kv_p), full(w_kr_p), full(vone)],
        out_specs=[row(NA_WIDTH)] * 3 + [row(width)] * 3 + [pl.BlockSpec((width, tb), lambda i: (0, i))],
        out_shape=outs,
        compiler_params=_cparams(("parallel",)),
        name="attn_in",
    )(h, cos_t, sin_t, g_attn, w_in_p, g_q, w_q_p, g_kv, w_kv_p, w_kr_p, vone)


def _na_kernel(cidx_ref, var_ref, midx_ref, q_ref, kp_ref, kc_ref, kn_ref, vp_ref, vc_ref, vn_ref,
               km_ref, vm_ref, bias_ref, oin_ref, o_ref):
    del cidx_ref, var_ref, midx_ref, oin_ref
    q = q_ref[...]
    lane = lax.broadcasted_iota(jnp.int32, (1, LANES), 1)
    ks = (kp_ref[...], kc_ref[...], kn_ref[...])
    vs = (vp_ref[...], vc_ref[...], vn_ref[...])
    km = km_ref[...]
    vm = vm_ref[...]
    outs = []
    for hh in range(2):
        in_head = (lane >= hh * NA_HEAD_DIM) & (lane < (hh + 1) * NA_HEAD_DIM)
        qh = jnp.where(in_head, q, jnp.zeros_like(q))
        s_loc = jnp.concatenate([_dot_nt(qh, k) for k in ks], axis=1) + bias_ref[0, hh]
        s_met = _dot_nt(qh, km)
        m = jnp.maximum(jnp.max(s_loc, axis=-1, keepdims=True), jnp.max(s_met, axis=-1, keepdims=True))
        p_loc = jnp.exp(s_loc - m)
        p_met = jnp.exp(s_met - m)
        l = jnp.sum(p_loc, axis=-1, keepdims=True) + jnp.sum(p_met, axis=-1, keepdims=True)
        o = _dot(p_met.astype(_BF16), vm)
        for j in range(NA_KBLOCKS):
            o = o + _dot(p_loc[:, j * NA_QBLOCK:(j + 1) * NA_QBLOCK].astype(_BF16), vs[j])
        outs.append(o / l)
    o_ref[...] = jnp.where(lane < NA_HEAD_DIM, outs[0], outs[1])


def _na_meta_kernel(midx_ref, q_ref, k_ref, v_ref, oin_ref, o_ref):
    del midx_ref, oin_ref
    q = q_ref[...]
    k = k_ref[...]
    v = v_ref[...]
    lane = lax.broadcasted_iota(jnp.int32, (1, LANES), 1)
    outs = []
    for hh in range(2):
        in_head = (lane >= hh * NA_HEAD_DIM) & (lane < (hh + 1) * NA_HEAD_DIM)
        s = _dot_nt(jnp.where(in_head, q, jnp.zeros_like(q)), k)
        p = jnp.exp(s - jnp.max(s, axis=-1, keepdims=True))
        outs.append(_dot(p.astype(_BF16), v) / jnp.sum(p, axis=-1, keepdims=True))
    o_ref[...] = jnp.where(lane < NA_HEAD_DIM, outs[0], outs[1])


def _na_bias(rpb_l):
    n_kr = NA_KBLOCKS * NA_QROWS
    qr = np.arange(NA_QROWS)[:, None]
    kr = np.arange(n_kr)[None, :]
    qc = np.arange(GRID_W)[:, None]
    kc = np.arange(GRID_W)[None, :]
    cs = np.clip(qc - NA_WIN_W // 2, 0, GRID_W - NA_WIN_W)
    col_ok = (kc >= cs) & (kc < cs + NA_WIN_W)
    dc = np.clip(kc - qc + NA_WIN_W - 1, 0, 2 * NA_WIN_W - 2)
    sel_c = (np.arange(2 * NA_WIN_W - 1)[None, None, :] == dc[:, :, None]) & col_ok[:, :, None]
    nrows = 4 * n_kr
    sel_r, ok = [], []
    for r0, k0 in ((0, 0), (NA_QROWS, 0), (nrows - NA_QROWS, nrows - n_kr)):
        r = r0 + qr
        key_row = k0 + kr
        rs = np.clip(r - NA_WIN_H // 2, 0, nrows - NA_WIN_H)
        row_ok = (key_row >= rs) & (key_row < rs + NA_WIN_H)
        dr = np.clip(key_row - r + NA_WIN_H - 1, 0, 2 * NA_WIN_H - 2)
        sel_r.append((np.arange(2 * NA_WIN_H - 1)[None, None, :] == dr[:, :, None]) & row_ok[:, :, None])
        ok.append(row_ok[:, None, :, None] & col_ok[None, :, None, :])
    sel_r = jnp.asarray(np.stack(sel_r), _F32)
    sel_c = jnp.asarray(sel_c, _F32)
    b = jnp.einsum("vqkd,hde,cxe->vhqckx", sel_r, rpb_l.astype(_F32), sel_c, precision=lax.Precision.HIGHEST)
    b = jnp.where(jnp.asarray(np.stack(ok))[:, None], b, NEG_BIG)
    return b.reshape(3, NA_HEADS, NA_QBLOCK, NA_KBLOCKS * NA_QBLOCK)


def _na_attention(q, k, v, bias, layout):
    rt = q.shape[0]
    cidx, var, midx = layout["na_cidx"], layout["na_var"], layout["na_midx"]
    nblk = cidx.shape[0]
    qspec = pl.BlockSpec((NA_QBLOCK, LANES), lambda p, b, c, vr, m: (b, p))
    kspec = lambda d: pl.BlockSpec((NA_QBLOCK, LANES), lambda p, b, c, vr, m: (c[b] + d, p))
    mspec = pl.BlockSpec((N_META, LANES), lambda p, b, c, vr, m: (m[b], p))
    bspec = pl.BlockSpec((1, 2, NA_QBLOCK, NA_KBLOCKS * NA_QBLOCK), lambda p, b, c, vr, m: (vr[b], p, 0, 0))
    o = pl.pallas_call(
        _na_kernel,
        grid_spec=pltpu.PrefetchScalarGridSpec(
            num_scalar_prefetch=3,
            grid=(N_PAIRS, nblk),
            in_specs=[qspec, kspec(-1), kspec(0), kspec(1), kspec(-1), kspec(0), kspec(1), mspec, mspec, bspec,
                      pl.BlockSpec(memory_space=pl.ANY)],
            out_specs=qspec),
        out_shape=jax.ShapeDtypeStruct((rt, NA_WIDTH), _F32),
        input_output_aliases={13: 0},
        compiler_params=_cparams(("parallel", "parallel")),
        name="na_attn",
    )(cidx, var, midx, q, k, k, k, v, v, v, k, v, bias, jnp.zeros((rt, NA_WIDTH), _F32))
    smidx = layout["seq_midx"]
    mq = pl.BlockSpec((N_META, LANES), lambda p, s, m: (m[s], p))
    return pl.pallas_call(
        _na_meta_kernel,
        grid_spec=pltpu.PrefetchScalarGridSpec(
            num_scalar_prefetch=1,
            grid=(N_PAIRS, smidx.shape[0]),
            in_specs=[mq, mq, mq, pl.BlockSpec(memory_space=pl.ANY)],
            out_specs=mq),
        out_shape=jax.ShapeDtypeStruct((rt, NA_WIDTH), _F32),
        input_output_aliases={4: 0},
        compiler_params=_cparams(("parallel", "parallel")),
        name="na_meta",
    )(smidx, q, k, v, o)


def _online_update(m_prev, acc, s, v):
    w = s.shape[1]
    if w % LANES == 0:
        parts = [s[:, j * LANES:(j + 1) * LANES] for j in range(w // LANES)]
        mx = parts[0]
        for part in parts[1:]:
            mx = jnp.maximum(mx, part)
        m_new = jnp.maximum(m_prev, jnp.max(mx, axis=-1, keepdims=True))
        p = jnp.concatenate([jnp.exp2(part - m_new) for part in parts], axis=1)
    else:
        m_new = jnp.maximum(m_prev, jnp.max(s, axis=-1, keepdims=True))
        p = jnp.exp2(s - m_new[:, :w])
    acc = jnp.exp2(m_prev - m_new) * acc + _dot(p.astype(_BF16), v)
    return m_new, acc


def _mla_kernel(q_ref, kt_ref, v_ref, km_ref, vm_ref, oin_ref, o_ref, m_sc, acc_sc):
    del oin_ref
    t = pl.program_id(3)
    nt = pl.num_programs(3)
    tk = v_ref.shape[0]
    chunk = min(MLA_CHUNK, tk)

    @pl.when(t == 0)
    def _():
        m_sc[...] = jnp.full(m_sc.shape, NEG_BIG, _F32)
        acc_sc[...] = jnp.zeros(acc_sc.shape, _F32)

    for hh in range(2):
        tile = slice(hh * HEAD_TILE, (hh + 1) * HEAD_TILE)
        q = q_ref[:, tile]
        m, acc = m_sc[hh], acc_sc[hh]
        for c in range(tk // chunk):
            cols = slice(c * chunk, (c + 1) * chunk)
            m, acc = _online_update(m, acc, _dot(q, kt_ref[tile, cols]), v_ref[cols, tile])
        m_sc[hh] = m
        acc_sc[hh] = acc

    @pl.when(t == nt - 1)
    def _():
        outs = []
        for hh in range(2):
            tile = slice(hh * HEAD_TILE, (hh + 1) * HEAD_TILE)
            _, acc = _online_update(m_sc[hh], acc_sc[hh], _dot_nt(q_ref[:, tile], km_ref[:, tile]), vm_ref[:, tile])
            outs.append(acc)
        lane = lax.broadcasted_iota(jnp.int32, (1, LANES), 1)
        l0 = outs[0][:, V_HEAD:V_HEAD + 1]
        l1 = outs[1][:, 0:1]
        o_ref[...] = jnp.where(lane < V_HEAD, outs[0] / l0, outs[1] / l1)


def _mla_call(q, kt, k, v, o_prev, *, tq, tk, n_seq, q_blk0, q_blk_stride, n_qblk, kv_blk0, kv_blk_stride, n_kvblk,
              meta_blk0, name):
    rt = q.shape[0]
    pw = 2 * HEAD_TILE
    qspec = pl.BlockSpec((tq, pw), lambda s, p, i, t: (q_blk0 + s * q_blk_stride + i, p))
    ktspec = pl.BlockSpec((pw, tk), lambda s, p, i, t: (p, kv_blk0 + s * kv_blk_stride + t))
    vspec = pl.BlockSpec((tk, pw), lambda s, p, i, t: (kv_blk0 + s * kv_blk_stride + t, p))
    mspec = pl.BlockSpec((N_META, pw), lambda s, p, i, t: (meta_blk0 + s, p))
    ospec = pl.BlockSpec((tq, LANES), lambda s, p, i, t: (q_blk0 + s * q_blk_stride + i, p))
    return pl.pallas_call(
        _mla_kernel,
        grid=(n_seq, N_PAIRS, n_qblk, n_kvblk),
        in_specs=[qspec, ktspec, vspec, mspec, mspec, pl.BlockSpec(memory_space=pl.ANY)],
        out_specs=ospec,
        out_shape=jax.ShapeDtypeStruct((rt, MLA_WIDTH), _F32),
        scratch_shapes=[pltpu.VMEM((2, tq, LANES), _F32), pltpu.VMEM((2, tq, LANES), _F32)],
        input_output_aliases={5: 0},
        compiler_params=_cparams(("parallel", "parallel", "parallel", "arbitrary")),
        name=name,
    )(q, kt, v, k, v, o_prev)


def _mla_attention(q, kt, k, v, layout):
    rt = q.shape[0]
    o = jnp.zeros((rt, MLA_WIDTH), _F32)
    nt = layout["n_tok_total"]
    for gi, (seq0, n_seq, n_tok) in enumerate(layout["groups"]):
        tq = min(MLA_TQ, n_tok)
        tk = min(MLA_TK, n_tok)
        off = layout["tok_off"][seq0]
        o = _mla_call(q, kt, k, v, o, tq=tq, tk=tk, n_seq=n_seq, q_blk0=off // tq, q_blk_stride=n_tok // tq,
                      n_qblk=n_tok // tq, kv_blk0=off // tk, kv_blk_stride=n_tok // tk, n_kvblk=n_tok // tk,
                      meta_blk0=nt // N_META + seq0, name=f"mla_tok{gi}")
        o = _mla_call(q, kt, k, v, o, tq=N_META, tk=tk, n_seq=n_seq, q_blk0=nt // N_META + seq0, q_blk_stride=1,
                      n_qblk=1, kv_blk0=off // tk, kv_blk_stride=n_tok // tk, n_kvblk=n_tok // tk,
                      meta_blk0=nt // N_META + seq0, name=f"mla_meta{gi}")
    return o


def _attn_out_kernel(ona_ref, omla_ref, h_ref, g_na_ref, g_mla_ref, w_out_ref, g_ffn_ref, w_r_ref, b_r_ref,
                     h1_ref, m_ref, idx_ref, pos_ref, gate_ref, cnt_ref, cnt_sc, *, n_valid):
    i = pl.program_id(0)
    tb = h_ref.shape[0]

    @pl.when(i == 0)
    def _():
        cnt_sc[...] = jnp.zeros(cnt_sc.shape, _F32)

    n1 = _rms(ona_ref[...], g_na_ref[...]).astype(_BF16)
    n2 = _rms(omla_ref[...], g_mla_ref[...]).astype(_BF16)
    h1 = h_ref[...] + _dot(n1, w_out_ref[0:NA_WIDTH, :]) + _dot(n2, w_out_ref[NA_WIDTH:, :])
    h1_ref[...] = h1
    m = _rms(h1, g_ffn_ref[...])
    m_ref[...] = m.reshape(m_ref.shape)
    m_hi = m.astype(_BF16)
    m_lo = (m - m_hi.astype(_F32)).astype(_BF16)
    hi = _dot(m_hi, w_r_ref[...])
    logits = hi[:, :LANES] + (hi[:, LANES:] + _dot(m_lo, w_r_ref[:, :LANES])) + b_r_ref[...]

    lane = lax.broadcasted_iota(jnp.int32, (tb, LANES), 1).astype(_F32)
    row = lax.broadcasted_iota(jnp.int32, (tb, 1), 0) + i * tb
    valid = jnp.where(row < n_valid, 1.0, 0.0)
    work = logits
    sel = jnp.zeros((tb, LANES), _F32)
    idx_out = jnp.zeros((tb, LANES), _F32)
    top = []
    for kk in range(TOP_K):
        mx = jnp.max(work, axis=-1, keepdims=True)
        idx = jnp.min(jnp.where(work == mx, lane, float(LANES)), axis=-1, keepdims=True)
        hit = lane == idx
        sel = jnp.where(hit, 1.0, sel)
        work = jnp.where(hit, NEG_BIG * 2, work)
        idx_out = jnp.where(lane == kk, idx, idx_out)
        top.append((mx, idx))
    e = [jnp.exp(mx - top[0][0]) for mx, _ in top]
    denom = e[0] + e[1] + e[2] + e[3]
    gate_out = jnp.zeros((tb, LANES), _F32)
    for kk in range(TOP_K):
        gate_out = jnp.where(lane == kk, e[kk] / denom, gate_out)
    gate_ref[...] = gate_out * valid
    idx_ref[...] = idx_out

    sel = sel * valid
    r_i = lax.broadcasted_iota(jnp.int32, (tb, tb), 0)
    c_i = lax.broadcasted_iota(jnp.int32, (tb, tb), 1)
    tri = jnp.where(c_i < r_i, 1.0, 0.0).astype(_BF16)
    pos_full = _dot(tri, sel.astype(_BF16)) + cnt_sc[...]
    pos_out = jnp.zeros((tb, LANES), _F32)
    for kk in range(TOP_K):
        pk = jnp.sum(jnp.where(lane == top[kk][1], pos_full, 0.0), axis=-1, keepdims=True)
        pos_out = jnp.where(lane == kk, pk, pos_out)
    pos_ref[...] = pos_out
    cnt_sc[...] = cnt_sc[...] + jnp.sum(sel, axis=0, keepdims=True)
    cnt_ref[...] = cnt_sc[...]


def _attn_out(o_na, o_mla, h, g_na, g_mla, w_out, g_ffn, w_r, b_r, n_valid):
    rt = h.shape[0]
    tb = ROW_BLOCK
    row = lambda w: pl.BlockSpec((tb, w), lambda i: (i, 0))
    full = lambda a: pl.BlockSpec(a.shape, lambda i: (0,) * a.ndim)
    return pl.pallas_call(
        functools.partial(_attn_out_kernel, n_valid=n_valid),
        grid=(rt // tb,),
        in_specs=[row(NA_WIDTH), row(MLA_WIDTH), row(D_MODEL), full(g_na), full(g_mla), full(w_out), full(g_ffn),
                  full(w_r), full(b_r)],
        out_specs=[row(D_MODEL), pl.BlockSpec((tb,) + ROW_TILE, lambda i: (i, 0, 0)), row(LANES), row(LANES),
                   row(LANES), pl.BlockSpec((1, LANES), lambda i: (0, 0))],
        out_shape=[jax.ShapeDtypeStruct((rt, D_MODEL), _F32), jax.ShapeDtypeStruct((rt,) + ROW_TILE, _F32),
                   jax.ShapeDtypeStruct((rt, LANES), _F32), jax.ShapeDtypeStruct((rt, LANES), _F32),
                   jax.ShapeDtypeStruct((rt, LANES), _F32), jax.ShapeDtypeStruct((1, LANES), _F32)],
        scratch_shapes=[pltpu.VMEM((1, LANES), _F32)],
        compiler_params=_cparams(("arbitrary",)),
        name="attn_out_router",
    )(o_na, o_mla, h, g_na, g_mla, w_out, g_ffn, w_r, b_r)


def _dest_kernel(idx_ref, pos_ref, pstart_ref, dest_ref, *, n_valid, trash0):
    i = pl.program_id(0)
    tb = idx_ref.shape[0]
    lane_i = lax.broadcasted_iota(jnp.int32, (tb, LANES), 1)
    lane = lane_i.astype(_F32)
    row = lax.broadcasted_iota(jnp.int32, (tb, 1), 0) + i * tb
    valid = row < n_valid
    idx = idx_ref[...]
    out = pos_ref[...]
    for kk in range(TOP_K):
        start = jnp.sum(jnp.where(lane == idx[:, kk:kk + 1], pstart_ref[...], 0.0), axis=-1, keepdims=True)
        out = jnp.where(lane == kk, out + start, out)
    dest_ref[...] = jnp.where(valid, out.astype(jnp.int32), trash0 + (row - n_valid) * TOP_K + lane_i)


def _dest(idx, pos, pstart, n_valid, trash0):
    rt = idx.shape[0]
    tb = ROW_BLOCK
    row = pl.BlockSpec((tb, LANES), lambda i: (i, 0))
    return pl.pallas_call(
        functools.partial(_dest_kernel, n_valid=n_valid, trash0=trash0),
        grid=(rt // tb,),
        in_specs=[row, row, pl.BlockSpec((1, LANES), lambda i: (0, 0))],
        out_specs=row,
        out_shape=jax.ShapeDtypeStruct((rt, LANES), jnp.int32),
        compiler_params=_cparams(("parallel",)),
        name="moe_dest",
    )(idx, pos, pstart)


def _row_copies_wait(ref, n_rows, sem):
    pltpu.make_async_copy(ref.at[pl.ds(0, n_rows)], ref.at[pl.ds(0, n_rows)], sem).wait()


def _dispatch_kernel(free_ref, dest_ref, m_ref, xs_ref, zero_sc, sem_free, sem_rows):
    i = pl.program_id(0)
    tb = m_ref.shape[0]
    n_free = free_ref.shape[0]

    @pl.when(i == 0)
    def _():
        zero_sc[...] = jnp.zeros(zero_sc.shape, zero_sc.dtype)

        def fill(j, carry):
            for u in range(2):
                pltpu.make_async_copy(zero_sc, xs_ref.at[free_ref[2 * j + u]], sem_free).start(priority=u)
            return carry

        lax.fori_loop(0, n_free // 2, fill, 0)
        _row_copies_wait(xs_ref, n_free, sem_free)

    def body(t, carry):
        for kk in range(TOP_K):
            d = dest_ref[0, 0, t * TOP_K + kk]
            pltpu.make_async_copy(m_ref.at[t], xs_ref.at[d], sem_rows).start(priority=kk % 2)
        return carry

    lax.fori_loop(0, tb, body, 0)
    _row_copies_wait(xs_ref, tb * TOP_K, sem_rows)


def _dispatch(free_slots, dest3, m, p_rows):
    rt = m.shape[0]
    tb = ROW_BLOCK
    return pl.pallas_call(
        _dispatch_kernel,
        grid_spec=pltpu.PrefetchScalarGridSpec(
            num_scalar_prefetch=1,
            grid=(rt // tb,),
            in_specs=[pl.BlockSpec((1, 1, tb * TOP_K), lambda i, f: (i, 0, 0), memory_space=pltpu.SMEM),
                      pl.BlockSpec((tb,) + ROW_TILE, lambda i, f: (i, 0, 0))],
            out_specs=pl.BlockSpec(memory_space=pl.ANY),
            scratch_shapes=[pltpu.VMEM(ROW_TILE, _F32), pltpu.SemaphoreType.DMA(()),
                            pltpu.SemaphoreType.DMA(())]),
        out_shape=jax.ShapeDtypeStruct((p_rows,) + ROW_TILE, _F32),
        compiler_params=_cparams(("arbitrary",)),
        name="moe_dispatch",
    )(free_slots, dest3, m)


def _ffn_kernel(be_ref, nu_ref, x_ref, wu_ref, bu_ref, wd_ref, bd_ref, y_ref, wu_sc, wd_sc):
    i = pl.program_id(0)

    @pl.when(i < nu_ref[0])
    def _():
        prev = be_ref[jnp.maximum(i - 1, 0)]

        @pl.when((i == 0) | (be_ref[i] != prev))
        def _():
            wu_sc[...] = wu_ref[0, 0].astype(_BF16)
            wd_sc[...] = wd_ref[0, 0].astype(_BF16)

        x = x_ref[...].reshape(x_ref.shape[0], D_MODEL)
        h = _dot(x.astype(_BF16), wu_sc[...]) + bu_ref[0, 0]
        gate = jnp.minimum(h[:, :D_FF], SWIGLU_LIMIT)
        up = jnp.clip(h[:, D_FF:], -SWIGLU_LIMIT, SWIGLU_LIMIT)
        glu = gate * jax.nn.sigmoid(gate * SWIGLU_ALPHA)
        y = _dot(((up + 1.0) * glu).astype(_BF16), wd_sc[...]) + bd_ref[0, 0]
        y_ref[...] = y.reshape(y_ref.shape)

    @pl.when(i >= nu_ref[0])
    def _():
        y_ref[...] = jnp.zeros(y_ref.shape, y_ref.dtype)


def _ffn(block_e, n_used, xs, w_up, b_up, w_down, b_down, n_blocks, layer):
    bm = EXPERT_BLOCK
    return pl.pallas_call(
        _ffn_kernel,
        grid_spec=pltpu.PrefetchScalarGridSpec(
            num_scalar_prefetch=2,
            grid=(n_blocks,),
            in_specs=[pl.BlockSpec((bm,) + ROW_TILE, lambda i, be, nu: (i, 0, 0)),
                      pl.BlockSpec((1, 1, D_MODEL, 2 * D_FF), lambda i, be, nu: (layer, be[i], 0, 0)),
                      pl.BlockSpec((1, 1, 1, 2 * D_FF), lambda i, be, nu: (layer, be[i], 0, 0)),
                      pl.BlockSpec((1, 1, D_FF, D_MODEL), lambda i, be, nu: (layer, be[i], 0, 0)),
                      pl.BlockSpec((1, 1, 1, D_MODEL), lambda i, be, nu: (layer, be[i], 0, 0))],
            out_specs=pl.BlockSpec((bm,) + ROW_TILE, lambda i, be, nu: (i, 0, 0)),
            scratch_shapes=[pltpu.VMEM((D_MODEL, 2 * D_FF), _BF16), pltpu.VMEM((D_FF, D_MODEL), _BF16)]),
        out_shape=jax.ShapeDtypeStruct((n_blocks * bm,) + ROW_TILE, _F32),
        compiler_params=_cparams(("arbitrary",)),
        name="moe_ffn",
    )(block_e, n_used, xs, w_up, b_up, w_down, b_down)


def _combine_kernel(dest_ref, gate_ref, h1_ref, g_ref, y_ref, o_ref, buf_sc, sem, *, final):
    tb = h1_ref.shape[0]

    def body(t, carry):
        for kk in range(TOP_K):
            d = dest_ref[0, 0, t * TOP_K + kk]
            pltpu.make_async_copy(y_ref.at[d], buf_sc.at[kk, t], sem).start(priority=kk % 2)
        return carry

    lax.fori_loop(0, tb, body, 0)
    _row_copies_wait(y_ref, tb * TOP_K, sem)
    gate = gate_ref[...]
    out = h1_ref[...]
    for kk in range(TOP_K):
        out = out + gate[:, kk:kk + 1] * buf_sc[kk].reshape(tb, D_MODEL)
    if final:
        out = _rms(out, g_ref[...])
    o_ref[...] = out


def _combine(dest3, gate, h1, g_final, y, final, blk0, nblk):
    tb = ROW_BLOCK
    return pl.pallas_call(
        functools.partial(_combine_kernel, final=final),
        grid=(nblk,),
        in_specs=[pl.BlockSpec((1, 1, tb * TOP_K), lambda i: (blk0 + i, 0, 0), memory_space=pltpu.SMEM),
                  pl.BlockSpec((tb, LANES), lambda i: (blk0 + i, 0)),
                  pl.BlockSpec((tb, D_MODEL), lambda i: (blk0 + i, 0)),
                  pl.BlockSpec((1, D_MODEL), lambda i: (0, 0)),
                  pl.BlockSpec(memory_space=pl.ANY)],
        out_specs=pl.BlockSpec((tb, D_MODEL), lambda i: (i, 0)),
        out_shape=jax.ShapeDtypeStruct((nblk * tb, D_MODEL), _F32),
        scratch_shapes=[pltpu.VMEM((TOP_K, tb) + ROW_TILE, _F32), pltpu.SemaphoreType.DMA(())],
        compiler_params=_cparams(("arbitrary",)),
        name="moe_combine",
    )(dest3, gate, h1, g_final, y)


def _make_layout(seq_tokens, groups):
    n_seq = len(seq_tokens)
    tok_off = np.concatenate([[0], np.cumsum(seq_tokens)]).astype(np.int64)
    nt = int(tok_off[-1])
    n_valid = nt + n_seq * N_META
    rt = -(-n_valid // ROW_BLOCK) * ROW_BLOCK
    pos = np.zeros((rt,), np.float32)
    cidx, var, midx = [], [], []
    for s, n in enumerate(seq_tokens):
        assert n % NA_QBLOCK == 0 and n // NA_QBLOCK >= NA_KBLOCKS
        pos[tok_off[s]:tok_off[s] + n] = N_META + np.arange(n)
        pos[nt + s * N_META:nt + (s + 1) * N_META] = np.arange(N_META)
        nb = n // NA_QBLOCK
        b0 = int(tok_off[s]) // NA_QBLOCK
        for b in range(nb):
            cidx.append(b0 + min(max(b, 1), nb - 2))
            var.append(0 if b == 0 else (2 if b == nb - 1 else 1))
            midx.append(nt // N_META + s)
    assert nt % ROW_BLOCK == 0
    return {
        "seq_tokens": tuple(seq_tokens), "groups": tuple(groups), "tok_off": tuple(int(v) for v in tok_off),
        "n_tok_total": nt, "n_valid": n_valid, "rt": rt, "pos": pos,
        "na_cidx": jnp.asarray(cidx, jnp.int32), "na_var": jnp.asarray(var, jnp.int32),
        "na_midx": jnp.asarray(midx, jnp.int32),
        "seq_midx": jnp.asarray([nt // N_META + s for s in range(n_seq)], jnp.int32),
    }


def _rope_tables(pos):
    freqs = jnp.power(ROPE_THETA, -jnp.arange(0, QK_ROPE, 2, dtype=_F32) / QK_ROPE)
    ang = jnp.asarray(pos)[:, None] * freqs[None, :]
    cos, sin = jnp.cos(ang), jnp.sin(ang)
    rt = pos.shape[0]
    pad = LANES - QK_NOPE - QK_ROPE
    cos_t = jnp.concatenate([jnp.ones((rt, QK_NOPE), _F32), cos, cos, jnp.zeros((rt, pad), _F32)], axis=1)
    sin_t = jnp.concatenate([jnp.zeros((rt, QK_NOPE), _F32), sin, sin, jnp.zeros((rt, pad), _F32)], axis=1)
    return cos_t, sin_t


def _layer_weights(w_in, w_uq, w_ukv):
    half = QK_ROPE // 2
    s2 = 3 * NA_WIDTH
    kr_cols = w_in[:, s2 + Q_LORA + KV_LORA:]
    w_in_p = jnp.concatenate([w_in[:, :NA_WIDTH] * (NA_HEAD_DIM ** -0.5), w_in[:, NA_WIDTH:s2 + Q_LORA + KV_LORA],
                              kr_cols, jnp.zeros((D_MODEL, LANES - QK_ROPE), _F32)], axis=1).astype(_BF16)
    dq = QK_NOPE + QK_ROPE
    wq = w_uq.reshape(Q_LORA, MLA_HEADS, dq)
    zq = jnp.zeros((Q_LORA, MLA_HEADS, LANES - dq), _F32)
    q_plain = jnp.concatenate([wq, zq], axis=2)
    q_rot = jnp.concatenate([jnp.zeros((Q_LORA, MLA_HEADS, QK_NOPE), _F32), -wq[:, :, QK_NOPE + half:],
                             wq[:, :, QK_NOPE:QK_NOPE + half], zq], axis=2)
    w_q_p = jnp.concatenate([q_plain.reshape(Q_LORA, -1), q_rot.reshape(Q_LORA, -1)], axis=1).astype(_BF16)
    wkv = w_ukv.reshape(KV_LORA, MLA_HEADS, QK_NOPE + V_HEAD)
    k_plain = jnp.concatenate([wkv[:, :, :QK_NOPE], jnp.zeros((KV_LORA, MLA_HEADS, LANES - QK_NOPE), _F32)], axis=2)
    wv = wkv[:, :, QK_NOPE:].reshape(KV_LORA, N_PAIRS, 2, V_HEAD)
    zv = jnp.zeros((KV_LORA, N_PAIRS, LANES - V_HEAD), _F32)
    v_even = jnp.concatenate([wv[:, :, 0], zv], axis=2)
    v_odd = jnp.concatenate([zv, wv[:, :, 1]], axis=2)
    v_plain = jnp.stack([v_even, v_odd], axis=2)
    w_kv_p = jnp.concatenate([k_plain.reshape(KV_LORA, -1), v_plain.reshape(KV_LORA, -1)], axis=1).astype(_BF16)
    return w_in_p, w_q_p, w_kv_p


def _const_tables():
    half = QK_ROPE // 2
    width = MLA_HEADS * HEAD_TILE
    r_plain = np.zeros((LANES, width), np.float32)
    r_rot = np.zeros((LANES, width), np.float32)
    vone = np.zeros((1, width), np.float32)
    for h in range(MLA_HEADS):
        base = h * HEAD_TILE + QK_NOPE
        for j in range(QK_ROPE):
            r_plain[j, base + j] = 1.0
        for j in range(half):
            r_rot[half + j, base + j] = -1.0
            r_rot[j, base + half + j] = 1.0
        vone[0, h * HEAD_TILE + (V_HEAD if h % 2 == 0 else 0)] = 1.0
    w_kr_p = jnp.asarray(np.concatenate([r_plain, r_rot], axis=1), _BF16)
    return w_kr_p, jnp.asarray(vone)


def _moe_plan(counts, n_blocks, n_free):
    counts = counts[0, :N_EXPERTS].astype(jnp.int32)
    padded = (counts + EXPERT_BLOCK - 1) // EXPERT_BLOCK * EXPERT_BLOCK
    pend = jnp.cumsum(padded)
    pstart = pend - padded
    first_row = jnp.arange(n_blocks, dtype=jnp.int32) * EXPERT_BLOCK
    block_e = jnp.minimum(jnp.sum((pend[None, :] <= first_row[:, None]).astype(jnp.int32), axis=1), N_EXPERTS - 1)
    n_used = (pend[-1:] // EXPERT_BLOCK).astype(jnp.int32)
    pstart_row = jnp.zeros((1, LANES), _F32).at[0, :N_EXPERTS].set(pstart.astype(_F32))
    gap = jnp.concatenate([padded - counts, (n_blocks * EXPERT_BLOCK - pend[-1:])])
    gap_end = jnp.cumsum(gap)
    gap_first = jnp.concatenate([pstart + counts, pend[-1:]])
    j = jnp.arange(n_free, dtype=jnp.int32)
    seg = jnp.sum((gap_end[None, :] <= j[:, None]).astype(jnp.int32), axis=1)
    free_slots = (gap_first[seg] + j - (gap_end - gap)[seg]).astype(jnp.int32)
    return pstart_row, block_e, n_used, free_slots


def _forward(h, layout, meta_tokens, g_attn, w_in, g_q, w_uq, g_kv, w_ukv, rpb, g_out_na, g_out_mla, w_out,
             g_ffn, w_router, b_router, w_up, b_up, w_down, b_down, g_final):
    del meta_tokens
    depth = w_in.shape[0]
    rt = layout["rt"]
    n_valid = layout["n_valid"]
    cos_t, sin_t = _rope_tables(layout["pos"])
    w_kr_p, vone = _const_tables()
    n_assign = n_valid * TOP_K
    n_blocks = -(-n_assign // EXPERT_BLOCK) + N_EXPERTS
    trash0 = n_blocks * EXPERT_BLOCK
    p_rows = trash0 + (rt - n_valid) * TOP_K
    n_free = trash0 - n_assign
    row2 = lambda a: a.reshape(1, -1)
    for l in range(depth):
        w_in_p, w_q_p, w_kv_p = _layer_weights(w_in[l], w_uq[l], w_ukv[l])
        qna, kna, vna, qm, km, vm, kmt = _attn_in(h, cos_t, sin_t, row2(g_attn[l]), w_in_p, row2(g_q[l]), w_q_p,
                                             row2(g_kv[l]), w_kv_p, w_kr_p, vone)
        o_na = _na_attention(qna, kna, vna, _na_bias(rpb[l]), layout)
        o_mla = _mla_attention(qm, kmt, km, vm, layout)
        w_r32 = jnp.concatenate([w_router[l], jnp.zeros((D_MODEL, LANES - N_EXPERTS), _F32)], axis=1)
        w_r_hi = w_r32.astype(_BF16)
        w_r = jnp.concatenate([w_r_hi, (w_r32 - w_r_hi.astype(_F32)).astype(_BF16)], axis=1)
        b_r = jnp.concatenate([b_router[l], jnp.full((LANES - N_EXPERTS,), NEG_BIG, _F32)]).reshape(1, LANES)
        h1, m, idx, pos, gate, counts = _attn_out(o_na, o_mla, h, row2(g_out_na[l]), row2(g_out_mla[l]),
                                                  w_out[l].astype(_BF16), row2(g_ffn[l]), w_r, b_r, n_valid)
        pstart_row, block_e, n_used, free_slots = _moe_plan(counts, n_blocks, n_free)
        dest = _dest(idx, pos, pstart_row, n_valid, trash0)
        dest3 = dest[:, :TOP_K].reshape(rt // ROW_BLOCK, 1, ROW_BLOCK * TOP_K)
        xs = _dispatch(free_slots, dest3, m, p_rows)
        y = _ffn(block_e, n_used, xs, w_up, b_up.reshape(depth, N_EXPERTS, 1, -1), w_down,
                 b_down.reshape(depth, N_EXPERTS, 1, -1), n_blocks, l)
        dest3c = jnp.where(dest3 >= trash0, 0, dest3)
        if l < depth - 1:
            h = _combine(dest3c, gate, h1, row2(g_final), y, False, 0, rt // ROW_BLOCK)
    outs = []
    for seq0, n_seq, n_tok in layout["groups"]:
        outs.append(_combine(dest3c, gate, h1, row2(g_final), y, True, layout["tok_off"][seq0] // ROW_BLOCK,
                             n_seq * n_tok // ROW_BLOCK))
    return outs


def kernel(x_prompt, x_sample, meta_tokens, g_attn, w_in, g_q, w_uq, g_kv, w_ukv, rpb, g_out_na, g_out_mla, w_out,
           g_ffn, w_router, b_router, w_up, b_up, w_down, b_down, g_final):
    bp, lp, _ = x_prompt.shape
    bs, ls, _ = x_sample.shape
    seq_tokens = [lp] * bp + [ls] * bs
    layout = _make_layout(seq_tokens, [(0, bp, lp), (bp, bs, ls)])
    n_seq = len(seq_tokens)
    rt, n_valid = layout["rt"], layout["n_valid"]
    meta = jnp.broadcast_to(meta_tokens[None], (n_seq, N_META, D_MODEL)).reshape(n_seq * N_META, D_MODEL)
    h = jnp.concatenate([x_prompt.reshape(bp * lp, D_MODEL), x_sample.reshape(bs * ls, D_MODEL), meta,
                         jnp.zeros((rt - n_valid, D_MODEL), _F32)], axis=0)
    out = _forward(h, layout, meta_tokens, g_attn, w_in, g_q, w_uq, g_kv, w_ukv, rpb, g_out_na, g_out_mla, w_out,
                   g_ffn, w_router, b_router, w_up, b_up, w_down, b_down, g_final)
    return (out[0].reshape(bp, lp, D_MODEL), out[1].reshape(bs, ls, D_MODEL))
```

```python
import functools

import numpy as np
import jax
import jax.numpy as jnp
from jax import lax
from jax.experimental import pallas as pl
from jax.experimental.pallas import tpu as pltpu

D_MODEL = 1024
GRID_W = 64
N_META = 16
NA_HEADS = 8
NA_HEAD_DIM = 64
NA_WIN_H = 8
NA_WIN_W = 16
NA_WIDTH = NA_HEADS * NA_HEAD_DIM
MLA_HEADS = 8
QK_NOPE = 64
QK_ROPE = 32
V_HEAD = 64
Q_LORA = 256
KV_LORA = 128
ROPE_THETA = 10000.0
MLA_WIDTH = MLA_HEADS * V_HEAD
N_EXPERTS = 32
TOP_K = 4
D_FF = 1024
SWIGLU_LIMIT = 7.0
SWIGLU_ALPHA = 1.702
EPS = 1e-6

LANES = 128
HEAD_TILE = LANES
N_PAIRS = NA_HEADS // 2
ROW_BLOCK = 768
FINAL_BLOCK = 1024
NA_QROWS = 4
NA_QBLOCK = NA_QROWS * GRID_W
NA_KBLOCKS = 3
EXPERT_BLOCK = 256
ROW_TILE = (D_MODEL // LANES, LANES)
MLA_TQ = 512
MLA_TK = 2048
MLA_CHUNK = 2048
MLA_STEP_PAIRS = 2
NEG_BIG = -1e30
LOG2_E = 1.4426950408889634
VMEM_LIMIT = 56 * 1024 * 1024

_F32 = jnp.float32
_BF16 = jnp.bfloat16


def _cparams(sem):
    return pltpu.CompilerParams(dimension_semantics=sem, vmem_limit_bytes=VMEM_LIMIT)


def _rms(x, g):
    return x * lax.rsqrt(jnp.mean(x * x, axis=-1, keepdims=True) + EPS) * g


def _dot(a, b):
    return jnp.dot(a, b, preferred_element_type=_F32)


def _dot_nt(a, b):
    return lax.dot_general(a, b, (((1,), (1,)), ((), ())), preferred_element_type=_F32)


def _attn_in_kernel(h_ref, cos_ref, sin_ref, g_attn_ref, w_in_ref, g_q_ref, w_q_ref, g_kv_ref,
                    w_kv_ref, w_kr_ref, vone_ref,
                    qna_ref, kna_ref, vna_ref, qm_ref, km_ref, vm_ref, kmt_ref):
    a = _rms(h_ref[...], g_attn_ref[...]).astype(_BF16)
    proj = _dot(a, w_in_ref[...])
    qna_ref[...] = proj[:, 0:NA_WIDTH].astype(_BF16)
    kna_ref[...] = proj[:, NA_WIDTH:2 * NA_WIDTH].astype(_BF16)
    vna_ref[...] = proj[:, 2 * NA_WIDTH:3 * NA_WIDTH].astype(_BF16)
    s2 = 3 * NA_WIDTH
    cq = proj[:, s2:s2 + Q_LORA]
    ckv = proj[:, s2 + Q_LORA:s2 + Q_LORA + KV_LORA]
    kr = proj[:, s2 + Q_LORA + KV_LORA:]
    cos = jnp.concatenate([cos_ref[...]] * MLA_HEADS, axis=1)
    sin = jnp.concatenate([sin_ref[...]] * MLA_HEADS, axis=1)
    width = MLA_HEADS * HEAD_TILE
    q2 = _dot(_rms(cq, g_q_ref[...]).astype(_BF16), w_q_ref[...])
    scale = (QK_NOPE + QK_ROPE) ** -0.5 * LOG2_E
    qm_ref[...] = ((q2[:, :width] * cos + q2[:, width:] * sin) * scale).astype(_BF16)
    kv2 = _dot(_rms(ckv, g_kv_ref[...]).astype(_BF16), w_kv_ref[...])
    kr2 = _dot(kr.astype(_BF16), w_kr_ref[...])
    km = (kv2[:, :width] + kr2[:, :width]) * cos + kr2[:, width:] * sin
    km_ref[...] = km.astype(_BF16)
    kmt_ref[...] = km.T.astype(_BF16)
    vm_ref[...] = (kv2[:, width:] + vone_ref[...]).astype(_BF16)


def _attn_in(h, cos_t, sin_t, g_attn, w_in_p, g_q, w_q_p, g_kv, w_kv_p, w_kr_p, vone):
    rt = h.shape[0]
    tb = ROW_BLOCK
    width = MLA_HEADS * HEAD_TILE
    row = lambda w: pl.BlockSpec((tb, w), lambda i: (i, 0))
    full = lambda a: pl.BlockSpec(a.shape, lambda i: (0,) * a.ndim)
    outs = ([jax.ShapeDtypeStruct((rt, NA_WIDTH), _BF16)] * 3 + [jax.ShapeDtypeStruct((rt, width), _BF16)] * 3
            + [jax.ShapeDtypeStruct((width, rt), _BF16)])
    return pl.pallas_call(
        _attn_in_kernel,
        grid=(rt // tb,),
        in_specs=[row(D_MODEL), row(LANES), row(LANES), full(g_attn), full(w_in_p), full(g_q), full(w_q_p),
                  full(g_kv), full(w_kv_p), full(w_kr_p), full(vone)],
        out_specs=[row(NA_WIDTH)] * 3 + [row(width)] * 3 + [pl.BlockSpec((width, tb), lambda i: (0, i))],
        out_shape=outs,
        compiler_params=_cparams(("parallel",)),
        name="attn_in",
    )(h, cos_t, sin_t, g_attn, w_in_p, g_q, w_q_p, g_kv, w_kv_p, w_kr_p, vone)


def _na_kernel(cidx_ref, var_ref, midx_ref, q_ref, kp_ref, kc_ref, kn_ref, vp_ref, vc_ref, vn_ref,
               km_ref, vm_ref, bias_ref, oin_ref, o_ref):
    del cidx_ref, var_ref, midx_ref, oin_ref
    q = q_ref[...]
    lane = lax.broadcasted_iota(jnp.int32, (1, LANES), 1)
    ks = (kp_ref[...], kc_ref[...], kn_ref[...])
    vs = (vp_ref[...], vc_ref[...], vn_ref[...])
    km = km_ref[...]
    vm = vm_ref[...]
    outs = []
    for hh in range(2):
        in_head = (lane >= hh * NA_HEAD_DIM) & (lane < (hh + 1) * NA_HEAD_DIM)
        qh = jnp.where(in_head, q, jnp.zeros_like(q))
        s_loc = jnp.concatenate([_dot_nt(qh, k) for k in ks], axis=1) + bias_ref[0, hh]
        s_met = _dot_nt(qh, km)
        m = jnp.maximum(jnp.max(s_loc, axis=-1, keepdims=True), jnp.max(s_met, axis=-1, keepdims=True))
        p_loc = jnp.exp(s_loc - m)
        p_met = jnp.exp(s_met - m)
        l = jnp.sum(p_loc, axis=-1, keepdims=True) + jnp.sum(p_met, axis=-1, keepdims=True)
        o = _dot(p_met.astype(_BF16), vm)
        for j in range(NA_KBLOCKS):
            o = o + _dot(p_loc[:, j * NA_QBLOCK:(j + 1) * NA_QBLOCK].astype(_BF16), vs[j])
        outs.append(o / l)
    o_ref[...] = jnp.where(lane < NA_HEAD_DIM, outs[0], outs[1])


def _na_meta_kernel(midx_ref, q_ref, k_ref, v_ref, oin_ref, o_ref):
    del midx_ref, oin_ref
    q = q_ref[...]
    k = k_ref[...]
    v = v_ref[...]
    lane = lax.broadcasted_iota(jnp.int32, (1, LANES), 1)
    outs = []
    for hh in range(2):
        in_head = (lane >= hh * NA_HEAD_DIM) & (lane < (hh + 1) * NA_HEAD_DIM)
        s = _dot_nt(jnp.where(in_head, q, jnp.zeros_like(q)), k)
        p = jnp.exp(s - jnp.max(s, axis=-1, keepdims=True))
        outs.append(_dot(p.astype(_BF16), v) / jnp.sum(p, axis=-1, keepdims=True))
    o_ref[...] = jnp.where(lane < NA_HEAD_DIM, outs[0], outs[1])


def _na_bias(rpb_l):
    n_kr = NA_KBLOCKS * NA_QROWS
    qr = np.arange(NA_QROWS)[:, None]
    kr = np.arange(n_kr)[None, :]
    qc = np.arange(GRID_W)[:, None]
    kc = np.arange(GRID_W)[None, :]
    cs = np.clip(qc - NA_WIN_W // 2, 0, GRID_W - NA_WIN_W)
    col_ok = (kc >= cs) & (kc < cs + NA_WIN_W)
    dc = np.clip(kc - qc + NA_WIN_W - 1, 0, 2 * NA_WIN_W - 2)
    sel_c = (np.arange(2 * NA_WIN_W - 1)[None, None, :] == dc[:, :, None]) & col_ok[:, :, None]
    nrows = 4 * n_kr
    sel_r, ok = [], []
    for r0, k0 in ((0, 0), (NA_QROWS, 0), (nrows - NA_QROWS, nrows - n_kr)):
        r = r0 + qr
        key_row = k0 + kr
        rs = np.clip(r - NA_WIN_H // 2, 0, nrows - NA_WIN_H)
        row_ok = (key_row >= rs) & (key_row < rs + NA_WIN_H)
        dr = np.clip(key_row - r + NA_WIN_H - 1, 0, 2 * NA_WIN_H - 2)
        sel_r.append((np.arange(2 * NA_WIN_H - 1)[None, None, :] == dr[:, :, None]) & row_ok[:, :, None])
        ok.append(row_ok[:, None, :, None] & col_ok[None, :, None, :])
    sel_r = jnp.asarray(np.stack(sel_r), _F32)
    sel_c = jnp.asarray(sel_c, _F32)
    b = jnp.einsum("vqkd,hde,cxe->vhqckx", sel_r, rpb_l.astype(_F32), sel_c, precision=lax.Precision.HIGHEST)
    b = jnp.where(jnp.asarray(np.stack(ok))[:, None], b, NEG_BIG)
    return b.reshape(3, NA_HEADS, NA_QBLOCK, NA_KBLOCKS * NA_QBLOCK)


def _na_attention(q, k, v, bias, layout):
    rt = q.shape[0]
    cidx, var, midx = layout["na_cidx"], layout["na_var"], layout["na_midx"]
    nblk = cidx.shape[0]
    qspec = pl.BlockSpec((NA_QBLOCK, LANES), lambda p, b, c, vr, m: (b, p))
    kspec = lambda d: pl.BlockSpec((NA_QBLOCK, LANES), lambda p, b, c, vr, m: (c[b] + d, p))
    mspec = pl.BlockSpec((N_META, LANES), lambda p, b, c, vr, m: (m[b], p))
    bspec = pl.BlockSpec((1, 2, NA_QBLOCK, NA_KBLOCKS * NA_QBLOCK), lambda p, b, c, vr, m: (vr[b], p, 0, 0))
    o = pl.pallas_call(
        _na_kernel,
        grid_spec=pltpu.PrefetchScalarGridSpec(
            num_scalar_prefetch=3,
            grid=(N_PAIRS, nblk),
            in_specs=[qspec, kspec(-1), kspec(0), kspec(1), kspec(-1), kspec(0), kspec(1), mspec, mspec, bspec,
                      pl.BlockSpec(memory_space=pl.ANY)],
            out_specs=qspec),
        out_shape=jax.ShapeDtypeStruct((rt, NA_WIDTH), _F32),
        input_output_aliases={13: 0},
        compiler_params=_cparams(("parallel", "parallel")),
        name="na_attn",
    )(cidx, var, midx, q, k, k, k, v, v, v, k, v, bias, jnp.zeros((rt, NA_WIDTH), _F32))
    smidx = layout["seq_midx"]
    mq = pl.BlockSpec((N_META, LANES), lambda p, s, m: (m[s], p))
    return pl.pallas_call(
        _na_meta_kernel,
        grid_spec=pltpu.PrefetchScalarGridSpec(
            num_scalar_prefetch=1,
            grid=(N_PAIRS, smidx.shape[0]),
            in_specs=[mq, mq, mq, pl.BlockSpec(memory_space=pl.ANY)],
            out_specs=mq),
        out_shape=jax.ShapeDtypeStruct((rt, NA_WIDTH), _F32),
        input_output_aliases={4: 0},
        compiler_params=_cparams(("parallel", "parallel")),
        name="na_meta",
    )(smidx, q, k, v, o)


def _online_update(m_prev, acc, s, v):
    w = s.shape[1]
    if w % LANES == 0:
        parts = [s[:, j * LANES:(j + 1) * LANES] for j in range(w // LANES)]
        mx = parts[0]
        for part in parts[1:]:
            mx = jnp.maximum(mx, part)
        m_new = jnp.maximum(m_prev, jnp.max(mx, axis=-1, keepdims=True))
        p = jnp.concatenate([jnp.exp2(part - m_new) for part in parts], axis=1)
    else:
        m_new = jnp.maximum(m_prev, jnp.max(s, axis=-1, keepdims=True))
        p = jnp.exp2(s - m_new[:, :w])
    acc = jnp.exp2(m_prev - m_new) * acc + _dot(p.astype(_BF16), v)
    return m_new, acc


def _mla_kernel(q_ref, kt_ref, v_ref, km_ref, vm_ref, oin_ref, o_ref, m_sc, acc_sc):
    del oin_ref
    t = pl.program_id(3)
    nt = pl.num_programs(3)
    tk = v_ref.shape[0]
    chunk = min(MLA_CHUNK, tk)

    @pl.when(t == 0)
    def _():
        m_sc[...] = jnp.full(m_sc.shape, NEG_BIG, _F32)
        acc_sc[...] = jnp.zeros(acc_sc.shape, _F32)

    n_heads = m_sc.shape[0]
    for hh in range(n_heads):
        tile = slice(hh * HEAD_TILE, (hh + 1) * HEAD_TILE)
        q = q_ref[:, tile]
        m, acc = m_sc[hh], acc_sc[hh]
        for c in range(tk // chunk):
            cols = slice(c * chunk, (c + 1) * chunk)
            m, acc = _online_update(m, acc, _dot(q, kt_ref[tile, cols]), v_ref[cols, tile])
        m_sc[hh] = m
        acc_sc[hh] = acc

    @pl.when(t == nt - 1)
    def _():
        lane = lax.broadcasted_iota(jnp.int32, (1, LANES), 1)
        for pair in range(n_heads // 2):
            outs = []
            for hh in (2 * pair, 2 * pair + 1):
                tile = slice(hh * HEAD_TILE, (hh + 1) * HEAD_TILE)
                _, acc = _online_update(m_sc[hh], acc_sc[hh], _dot_nt(q_ref[:, tile], km_ref[:, tile]),
                                        vm_ref[:, tile])
                outs.append(acc)
            l0 = outs[0][:, V_HEAD:V_HEAD + 1]
            l1 = outs[1][:, 0:1]
            o_ref[:, pair * LANES:(pair + 1) * LANES] = jnp.where(lane < V_HEAD, outs[0] / l0, outs[1] / l1)


def _mla_call(q, kt, k, v, o_prev, *, tq, tk, n_seq, q_blk0, q_blk_stride, n_qblk, kv_blk0, kv_blk_stride, n_kvblk,
              meta_blk0, name):
    rt = q.shape[0]
    n_heads = 2 * MLA_STEP_PAIRS
    pw = n_heads * HEAD_TILE
    qspec = pl.BlockSpec((tq, pw), lambda s, p, i, t: (q_blk0 + s * q_blk_stride + i, p))
    ktspec = pl.BlockSpec((pw, tk), lambda s, p, i, t: (p, kv_blk0 + s * kv_blk_stride + t))
    vspec = pl.BlockSpec((tk, pw), lambda s, p, i, t: (kv_blk0 + s * kv_blk_stride + t, p))
    mspec = pl.BlockSpec((N_META, pw), lambda s, p, i, t: (meta_blk0 + s, p))
    ospec = pl.BlockSpec((tq, MLA_STEP_PAIRS * LANES), lambda s, p, i, t: (q_blk0 + s * q_blk_stride + i, p))
    return pl.pallas_call(
        _mla_kernel,
        grid=(n_seq, N_PAIRS // MLA_STEP_PAIRS, n_qblk, n_kvblk),
        in_specs=[qspec, ktspec, vspec, mspec, mspec, pl.BlockSpec(memory_space=pl.ANY)],
        out_specs=ospec,
        out_shape=jax.ShapeDtypeStruct((rt, MLA_WIDTH), _F32),
        scratch_shapes=[pltpu.VMEM((n_heads, tq, LANES), _F32), pltpu.VMEM((n_heads, tq, LANES), _F32)],
        input_output_aliases={5: 0},
        compiler_params=_cparams(("parallel", "parallel", "parallel", "arbitrary")),
        name=name,
    )(q, kt, v, k, v, o_prev)


def _mla_attention(q, kt, k, v, layout):
    rt = q.shape[0]
    o = jnp.zeros((rt, MLA_WIDTH), _F32)
    nt = layout["n_tok_total"]
    for gi, (seq0, n_seq, n_tok) in enumerate(layout["groups"]):
        tq = min(MLA_TQ, n_tok)
        tk = min(MLA_TK, n_tok)
        off = layout["tok_off"][seq0]
        assert off % tq == 0 and off % tk == 0 and n_tok % tq == 0 and n_tok % tk == 0
        o = _mla_call(q, kt, k, v, o, tq=tq, tk=tk, n_seq=n_seq, q_blk0=off // tq, q_blk_stride=n_tok // tq,
                      n_qblk=n_tok // tq, kv_blk0=off // tk, kv_blk_stride=n_tok // tk, n_kvblk=n_tok // tk,
                      meta_blk0=nt // N_META + seq0, name=f"mla_tok{gi}")
        o = _mla_call(q, kt, k, v, o, tq=N_META, tk=tk, n_seq=n_seq, q_blk0=nt // N_META + seq0, q_blk_stride=1,
                      n_qblk=1, kv_blk0=off // tk, kv_blk_stride=n_tok // tk, n_kvblk=n_tok // tk,
                      meta_blk0=nt // N_META + seq0, name=f"mla_meta{gi}")
    return o


def _attn_out_kernel(ona_ref, omla_ref, h_ref, g_na_ref, g_mla_ref, w_out_ref, g_ffn_ref, w_r_ref, b_r_ref,
                     h1_ref, m_ref, idx_ref, pos_ref, gate_ref, cnt_ref, cnt_sc, *, n_valid):
    i = pl.program_id(0)
    tb = h_ref.shape[0]

    @pl.when(i == 0)
    def _():
        cnt_sc[...] = jnp.zeros(cnt_sc.shape, _F32)

    n1 = _rms(ona_ref[...], g_na_ref[...]).astype(_BF16)
    n2 = _rms(omla_ref[...], g_mla_ref[...]).astype(_BF16)
    h1 = h_ref[...] + _dot(n1, w_out_ref[0:NA_WIDTH, :]) + _dot(n2, w_out_ref[NA_WIDTH:, :])
    h1_ref[...] = h1
    m = _rms(h1, g_ffn_ref[...])
    m_ref[...] = m.reshape(m_ref.shape)
    m_hi = m.astype(_BF16)
    m_lo = (m - m_hi.astype(_F32)).astype(_BF16)
    hi = _dot(m_hi, w_r_ref[...])
    logits = hi[:, :LANES] + (hi[:, LANES:] + _dot(m_lo, w_r_ref[:, :LANES])) + b_r_ref[...]

    lane = lax.broadcasted_iota(jnp.int32, (tb, LANES), 1).astype(_F32)
    row = lax.broadcasted_iota(jnp.int32, (tb, 1), 0) + i * tb
    valid = jnp.where(row < n_valid, 1.0, 0.0)
    work = logits
    sel = jnp.zeros((tb, LANES), _F32)
    idx_out = jnp.zeros((tb, LANES), _F32)
    top = []
    for kk in range(TOP_K):
        mx = jnp.max(work, axis=-1, keepdims=True)
        idx = jnp.min(jnp.where(work == mx, lane, float(LANES)), axis=-1, keepdims=True)
        hit = lane == idx
        sel = jnp.where(hit, 1.0, sel)
        work = jnp.where(hit, NEG_BIG * 2, work)
        idx_out = jnp.where(lane == kk, idx, idx_out)
        top.append((mx, idx))
    e = [jnp.exp(mx - top[0][0]) for mx, _ in top]
    denom = e[0] + e[1] + e[2] + e[3]
    gate_out = jnp.zeros((tb, LANES), _F32)
    for kk in range(TOP_K):
        gate_out = jnp.where(lane == kk, e[kk] / denom, gate_out)
    gate_ref[...] = gate_out * valid
    idx_ref[...] = idx_out

    sel = sel * valid
    r_i = lax.broadcasted_iota(jnp.int32, (tb, tb), 0)
    c_i = lax.broadcasted_iota(jnp.int32, (tb, tb), 1)
    tri = jnp.where(c_i < r_i, 1.0, 0.0).astype(_BF16)
    pos_full = _dot(tri, sel.astype(_BF16)) + cnt_sc[...]
    pos_out = jnp.zeros((tb, LANES), _F32)
    for kk in range(TOP_K):
        pk = jnp.sum(jnp.where(lane == top[kk][1], pos_full, 0.0), axis=-1, keepdims=True)
        pos_out = jnp.where(lane == kk, pk, pos_out)
    pos_ref[...] = pos_out
    cnt_sc[...] = cnt_sc[...] + jnp.sum(sel, axis=0, keepdims=True)
    cnt_ref[...] = cnt_sc[...]


def _attn_out(o_na, o_mla, h, g_na, g_mla, w_out, g_ffn, w_r, b_r, n_valid):
    rt = h.shape[0]
    tb = ROW_BLOCK
    row = lambda w: pl.BlockSpec((tb, w), lambda i: (i, 0))
    full = lambda a: pl.BlockSpec(a.shape, lambda i: (0,) * a.ndim)
    return pl.pallas_call(
        functools.partial(_attn_out_kernel, n_valid=n_valid),
        grid=(rt // tb,),
        in_specs=[row(NA_WIDTH), row(MLA_WIDTH), row(D_MODEL), full(g_na), full(g_mla), full(w_out), full(g_ffn),
                  full(w_r), full(b_r)],
        out_specs=[row(D_MODEL), pl.BlockSpec((tb,) + ROW_TILE, lambda i: (i, 0, 0)), row(LANES), row(LANES),
                   row(LANES), pl.BlockSpec((1, LANES), lambda i: (0, 0))],
        out_shape=[jax.ShapeDtypeStruct((rt, D_MODEL), _F32), jax.ShapeDtypeStruct((rt,) + ROW_TILE, _F32),
                   jax.ShapeDtypeStruct((rt, LANES), _F32), jax.ShapeDtypeStruct((rt, LANES), _F32),
                   jax.ShapeDtypeStruct((rt, LANES), _F32), jax.ShapeDtypeStruct((1, LANES), _F32)],
        scratch_shapes=[pltpu.VMEM((1, LANES), _F32)],
        compiler_params=_cparams(("arbitrary",)),
        name="attn_out_router",
    )(o_na, o_mla, h, g_na, g_mla, w_out, g_ffn, w_r, b_r)


def _dest_kernel(idx_ref, pos_ref, pstart_ref, dest_ref, *, n_valid, trash0):
    i = pl.program_id(0)
    tb = idx_ref.shape[0]
    lane_i = lax.broadcasted_iota(jnp.int32, (tb, LANES), 1)
    lane = lane_i.astype(_F32)
    row = lax.broadcasted_iota(jnp.int32, (tb, 1), 0) + i * tb
    valid = row < n_valid
    idx = idx_ref[...]
    out = pos_ref[...]
    for kk in range(TOP_K):
        start = jnp.sum(jnp.where(lane == idx[:, kk:kk + 1], pstart_ref[...], 0.0), axis=-1, keepdims=True)
        out = jnp.where(lane == kk, out + start, out)
    dest_ref[...] = jnp.where(valid, out.astype(jnp.int32), trash0 + (row - n_valid) * TOP_K + lane_i)


def _dest(idx, pos, pstart, n_valid, trash0):
    rt = idx.shape[0]
    tb = ROW_BLOCK
    row = pl.BlockSpec((tb, LANES), lambda i: (i, 0))
    return pl.pallas_call(
        functools.partial(_dest_kernel, n_valid=n_valid, trash0=trash0),
        grid=(rt // tb,),
        in_specs=[row, row, pl.BlockSpec((1, LANES), lambda i: (0, 0))],
        out_specs=row,
        out_shape=jax.ShapeDtypeStruct((rt, LANES), jnp.int32),
        compiler_params=_cparams(("parallel",)),
        name="moe_dest",
    )(idx, pos, pstart)


def _row_copies_wait(ref, n_rows, sem):
    pltpu.make_async_copy(ref.at[pl.ds(0, n_rows)], ref.at[pl.ds(0, n_rows)], sem).wait()


def _dispatch_kernel(free_ref, dest_ref, m_ref, xs_ref, zero_sc, sem_free, sem_rows):
    i = pl.program_id(0)
    tb = m_ref.shape[0]
    n_free = free_ref.shape[0]

    @pl.when(i == 0)
    def _():
        zero_sc[...] = jnp.zeros(zero_sc.shape, zero_sc.dtype)

        def fill(j, carry):
            for u in range(2):
                pltpu.make_async_copy(zero_sc, xs_ref.at[free_ref[2 * j + u]], sem_free).start(priority=u)
            return carry

        lax.fori_loop(0, n_free // 2, fill, 0)
        _row_copies_wait(xs_ref, n_free, sem_free)

    def body(t, carry):
        for kk in range(TOP_K):
            d = dest_ref[0, 0, t * TOP_K + kk]
            pltpu.make_async_copy(m_ref.at[t], xs_ref.at[d], sem_rows).start(priority=kk % 2)
        return carry

    lax.fori_loop(0, tb, body, 0)
    _row_copies_wait(xs_ref, tb * TOP_K, sem_rows)


def _dispatch(free_slots, dest3, m, p_rows):
    rt = m.shape[0]
    tb = ROW_BLOCK
    return pl.pallas_call(
        _dispatch_kernel,
        grid_spec=pltpu.PrefetchScalarGridSpec(
            num_scalar_prefetch=1,
            grid=(rt // tb,),
            in_specs=[pl.BlockSpec((1, 1, tb * TOP_K), lambda i, f: (i, 0, 0), memory_space=pltpu.SMEM),
                      pl.BlockSpec((tb,) + ROW_TILE, lambda i, f: (i, 0, 0))],
            out_specs=pl.BlockSpec(memory_space=pl.ANY),
            scratch_shapes=[pltpu.VMEM(ROW_TILE, _F32), pltpu.SemaphoreType.DMA(()),
                            pltpu.SemaphoreType.DMA(())]),
        out_shape=jax.ShapeDtypeStruct((p_rows,) + ROW_TILE, _F32),
        compiler_params=_cparams(("arbitrary",)),
        name="moe_dispatch",
    )(free_slots, dest3, m)


def _ffn_kernel(be_ref, nu_ref, x_ref, wu_ref, bu_ref, wd_ref, bd_ref, y_ref, wu_sc, wd_sc):
    i = pl.program_id(0)

    @pl.when(i < nu_ref[0])
    def _():
        prev = be_ref[jnp.maximum(i - 1, 0)]

        @pl.when((i == 0) | (be_ref[i] != prev))
        def _():
            wu_sc[...] = wu_ref[0, 0].astype(_BF16)
            wd_sc[...] = wd_ref[0, 0].astype(_BF16)

        x = x_ref[...].reshape(x_ref.shape[0], D_MODEL)
        h = _dot(x.astype(_BF16), wu_sc[...]) + bu_ref[0, 0]
        gate = jnp.minimum(h[:, :D_FF], SWIGLU_LIMIT)
        up = jnp.clip(h[:, D_FF:], -SWIGLU_LIMIT, SWIGLU_LIMIT)
        glu = gate * jax.nn.sigmoid(gate * SWIGLU_ALPHA)
        y = _dot(((up + 1.0) * glu).astype(_BF16), wd_sc[...]) + bd_ref[0, 0]
        y_ref[...] = y.reshape(y_ref.shape)

    @pl.when(i >= nu_ref[0])
    def _():
        y_ref[...] = jnp.zeros(y_ref.shape, y_ref.dtype)


def _ffn(block_e, n_used, xs, w_up, b_up, w_down, b_down, n_blocks, layer):
    bm = EXPERT_BLOCK
    return pl.pallas_call(
        _ffn_kernel,
        grid_spec=pltpu.PrefetchScalarGridSpec(
            num_scalar_prefetch=2,
            grid=(n_blocks,),
            in_specs=[pl.BlockSpec((bm,) + ROW_TILE, lambda i, be, nu: (i, 0, 0)),
                      pl.BlockSpec((1, 1, D_MODEL, 2 * D_FF), lambda i, be, nu: (layer, be[i], 0, 0)),
                      pl.BlockSpec((1, 1, 1, 2 * D_FF), lambda i, be, nu: (layer, be[i], 0, 0)),
                      pl.BlockSpec((1, 1, D_FF, D_MODEL), lambda i, be, nu: (layer, be[i], 0, 0)),
                      pl.BlockSpec((1, 1, 1, D_MODEL), lambda i, be, nu: (layer, be[i], 0, 0))],
            out_specs=pl.BlockSpec((bm,) + ROW_TILE, lambda i, be, nu: (i, 0, 0)),
            scratch_shapes=[pltpu.VMEM((D_MODEL, 2 * D_FF), _BF16), pltpu.VMEM((D_FF, D_MODEL), _BF16)]),
        out_shape=jax.ShapeDtypeStruct((n_blocks * bm,) + ROW_TILE, _F32),
        compiler_params=_cparams(("arbitrary",)),
        name="moe_ffn",
    )(block_e, n_used, xs, w_up, b_up, w_down, b_down)


def _combine_kernel(dest_ref, gate_ref, h1_ref, g_ref, y_ref, o_ref, buf_sc, sem, *, final):
    tb = h1_ref.shape[0]

    def body(t, carry):
        for kk in range(TOP_K):
            d = dest_ref[0, 0, t * TOP_K + kk]
            pltpu.make_async_copy(y_ref.at[d], buf_sc.at[kk, t], sem).start(priority=kk % 2)
        return carry

    lax.fori_loop(0, tb, body, 0)
    _row_copies_wait(y_ref, tb * TOP_K, sem)
    gate = gate_ref[...]
    out = h1_ref[...]
    for kk in range(TOP_K):
        out = out + gate[:, kk:kk + 1] * buf_sc[kk].reshape(tb, D_MODEL)
    if final:
        out = _rms(out, g_ref[...])
    o_ref[...] = out


def _combine(dest4, gate, h1, g_final, y, final, tb, blk0, nblk):
    dest3 = dest4[blk0 * tb:(blk0 + nblk) * tb].reshape(nblk, 1, tb * TOP_K)
    return pl.pallas_call(
        functools.partial(_combine_kernel, final=final),
        grid=(nblk,),
        in_specs=[pl.BlockSpec((1, 1, tb * TOP_K), lambda i: (i, 0, 0), memory_space=pltpu.SMEM),
                  pl.BlockSpec((tb, LANES), lambda i: (blk0 + i, 0)),
                  pl.BlockSpec((tb, D_MODEL), lambda i: (blk0 + i, 0)),
                  pl.BlockSpec((1, D_MODEL), lambda i: (0, 0)),
                  pl.BlockSpec(memory_space=pl.ANY)],
        out_specs=pl.BlockSpec((tb, D_MODEL), lambda i: (i, 0)),
        out_shape=jax.ShapeDtypeStruct((nblk * tb, D_MODEL), _F32),
        scratch_shapes=[pltpu.VMEM((TOP_K, tb) + ROW_TILE, _F32), pltpu.SemaphoreType.DMA(())],
        compiler_params=_cparams(("arbitrary",)),
        name="moe_combine",
    )(dest3, gate, h1, g_final, y)


def _make_layout(seq_tokens, groups):
    n_seq = len(seq_tokens)
    tok_off = np.concatenate([[0], np.cumsum(seq_tokens)]).astype(np.int64)
    nt = int(tok_off[-1])
    n_valid = nt + n_seq * N_META
    rt = -(-n_valid // ROW_BLOCK) * ROW_BLOCK
    pos = np.zeros((rt,), np.float32)
    cidx, var, midx = [], [], []
    for s, n in enumerate(seq_tokens):
        assert n % NA_QBLOCK == 0 and n // NA_QBLOCK >= NA_KBLOCKS
        pos[tok_off[s]:tok_off[s] + n] = N_META + np.arange(n)
        pos[nt + s * N_META:nt + (s + 1) * N_META] = np.arange(N_META)
        nb = n // NA_QBLOCK
        b0 = int(tok_off[s]) // NA_QBLOCK
        for b in range(nb):
            cidx.append(b0 + min(max(b, 1), nb - 2))
            var.append(0 if b == 0 else (2 if b == nb - 1 else 1))
            midx.append(nt // N_META + s)
    return {
        "seq_tokens": tuple(seq_tokens), "groups": tuple(groups), "tok_off": tuple(int(v) for v in tok_off),
        "n_tok_total": nt, "n_valid": n_valid, "rt": rt, "pos": pos,
        "na_cidx": jnp.asarray(cidx, jnp.int32), "na_var": jnp.asarray(var, jnp.int32),
        "na_midx": jnp.asarray(midx, jnp.int32),
        "seq_midx": jnp.asarray([nt // N_META + s for s in range(n_seq)], jnp.int32),
    }


def _rope_tables(pos):
    freqs = jnp.power(ROPE_THETA, -jnp.arange(0, QK_ROPE, 2, dtype=_F32) / QK_ROPE)
    ang = jnp.asarray(pos)[:, None] * freqs[None, :]
    cos, sin = jnp.cos(ang), jnp.sin(ang)
    rt = pos.shape[0]
    pad = LANES - QK_NOPE - QK_ROPE
    cos_t = jnp.concatenate([jnp.ones((rt, QK_NOPE), _F32), cos, cos, jnp.zeros((rt, pad), _F32)], axis=1)
    sin_t = jnp.concatenate([jnp.zeros((rt, QK_NOPE), _F32), sin, sin, jnp.zeros((rt, pad), _F32)], axis=1)
    return cos_t, sin_t


def _layer_weights(w_in, w_uq, w_ukv):
    half = QK_ROPE // 2
    s2 = 3 * NA_WIDTH
    kr_cols = w_in[:, s2 + Q_LORA + KV_LORA:]
    w_in_p = jnp.concatenate([w_in[:, :NA_WIDTH] * (NA_HEAD_DIM ** -0.5), w_in[:, NA_WIDTH:s2 + Q_LORA + KV_LORA],
                              kr_cols, jnp.zeros((D_MODEL, LANES - QK_ROPE), _F32)], axis=1).astype(_BF16)
    dq = QK_NOPE + QK_ROPE
    wq = w_uq.reshape(Q_LORA, MLA_HEADS, dq)
    zq = jnp.zeros((Q_LORA, MLA_HEADS, LANES - dq), _F32)
    q_plain = jnp.concatenate([wq, zq], axis=2)
    q_rot = jnp.concatenate([jnp.zeros((Q_LORA, MLA_HEADS, QK_NOPE), _F32), -wq[:, :, QK_NOPE + half:],
                             wq[:, :, QK_NOPE:QK_NOPE + half], zq], axis=2)
    w_q_p = jnp.concatenate([q_plain.reshape(Q_LORA, -1), q_rot.reshape(Q_LORA, -1)], axis=1).astype(_BF16)
    wkv = w_ukv.reshape(KV_LORA, MLA_HEADS, QK_NOPE + V_HEAD)
    k_plain = jnp.concatenate([wkv[:, :, :QK_NOPE], jnp.zeros((KV_LORA, MLA_HEADS, LANES - QK_NOPE), _F32)], axis=2)
    wv = wkv[:, :, QK_NOPE:].reshape(KV_LORA, N_PAIRS, 2, V_HEAD)
    zv = jnp.zeros((KV_LORA, N_PAIRS, LANES - V_HEAD), _F32)
    v_even = jnp.concatenate([wv[:, :, 0], zv], axis=2)
    v_odd = jnp.concatenate([zv, wv[:, :, 1]], axis=2)
    v_plain = jnp.stack([v_even, v_odd], axis=2)
    w_kv_p = jnp.concatenate([k_plain.reshape(KV_LORA, -1), v_plain.reshape(KV_LORA, -1)], axis=1).astype(_BF16)
    return w_in_p, w_q_p, w_kv_p


def _const_tables():
    half = QK_ROPE // 2
    width = MLA_HEADS * HEAD_TILE
    r_plain = np.zeros((LANES, width), np.float32)
    r_rot = np.zeros((LANES, width), np.float32)
    vone = np.zeros((1, width), np.float32)
    for h in range(MLA_HEADS):
        base = h * HEAD_TILE + QK_NOPE
        for j in range(QK_ROPE):
            r_plain[j, base + j] = 1.0
        for j in range(half):
            r_rot[half + j, base + j] = -1.0
            r_rot[j, base + half + j] = 1.0
        vone[0, h * HEAD_TILE + (V_HEAD if h % 2 == 0 else 0)] = 1.0
    w_kr_p = jnp.asarray(np.concatenate([r_plain, r_rot], axis=1), _BF16)
    return w_kr_p, jnp.asarray(vone)


def _moe_plan(counts, n_blocks, n_free):
    counts = counts[0, :N_EXPERTS].astype(jnp.int32)
    padded = (counts + EXPERT_BLOCK - 1) // EXPERT_BLOCK * EXPERT_BLOCK
    pend = jnp.cumsum(padded)
    pstart = pend - padded
    first_row = jnp.arange(n_blocks, dtype=jnp.int32) * EXPERT_BLOCK
    block_e = jnp.minimum(jnp.sum((pend[None, :] <= first_row[:, None]).astype(jnp.int32), axis=1), N_EXPERTS - 1)
    n_used = (pend[-1:] // EXPERT_BLOCK).astype(jnp.int32)
    pstart_row = jnp.zeros((1, LANES), _F32).at[0, :N_EXPERTS].set(pstart.astype(_F32))
    gap = jnp.concatenate([padded - counts, (n_blocks * EXPERT_BLOCK - pend[-1:])])
    gap_end = jnp.cumsum(gap)
    gap_first = jnp.concatenate([pstart + counts, pend[-1:]])
    j = jnp.arange(n_free, dtype=jnp.int32)
    seg = jnp.sum((gap_end[None, :] <= j[:, None]).astype(jnp.int32), axis=1)
    free_slots = (gap_first[seg] + j - (gap_end - gap)[seg]).astype(jnp.int32)
    return pstart_row, block_e, n_used, free_slots


def _forward(h, layout, meta_tokens, g_attn, w_in, g_q, w_uq, g_kv, w_ukv, rpb, g_out_na, g_out_mla, w_out,
             g_ffn, w_router, b_router, w_up, b_up, w_down, b_down, g_final):
    del meta_tokens
    depth = w_in.shape[0]
    rt = layout["rt"]
    n_valid = layout["n_valid"]
    cos_t, sin_t = _rope_tables(layout["pos"])
    w_kr_p, vone = _const_tables()
    n_assign = n_valid * TOP_K
    n_blocks = -(-n_assign // EXPERT_BLOCK) + N_EXPERTS
    trash0 = n_blocks * EXPERT_BLOCK
    p_rows = trash0 + (rt - n_valid) * TOP_K
    n_free = trash0 - n_assign
    row2 = lambda a: a.reshape(1, -1)
    for l in range(depth):
        w_in_p, w_q_p, w_kv_p = _layer_weights(w_in[l], w_uq[l], w_ukv[l])
        qna, kna, vna, qm, km, vm, kmt = _attn_in(h, cos_t, sin_t, row2(g_attn[l]), w_in_p, row2(g_q[l]), w_q_p,
                                             row2(g_kv[l]), w_kv_p, w_kr_p, vone)
        o_na = _na_attention(qna, kna, vna, _na_bias(rpb[l]), layout)
        o_mla = _mla_attention(qm, kmt, km, vm, layout)
        w_r32 = jnp.concatenate([w_router[l], jnp.zeros((D_MODEL, LANES - N_EXPERTS), _F32)], axis=1)
        w_r_hi = w_r32.astype(_BF16)
        w_r = jnp.concatenate([w_r_hi, (w_r32 - w_r_hi.astype(_F32)).astype(_BF16)], axis=1)
        b_r = jnp.concatenate([b_router[l], jnp.full((LANES - N_EXPERTS,), NEG_BIG, _F32)]).reshape(1, LANES)
        h1, m, idx, pos, gate, counts = _attn_out(o_na, o_mla, h, row2(g_out_na[l]), row2(g_out_mla[l]),
                                                  w_out[l].astype(_BF16), row2(g_ffn[l]), w_r, b_r, n_valid)
        pstart_row, block_e, n_used, free_slots = _moe_plan(counts, n_blocks, n_free)
        dest = _dest(idx, pos, pstart_row, n_valid, trash0)
        dest4 = dest[:, :TOP_K]
        xs = _dispatch(free_slots, dest4.reshape(rt // ROW_BLOCK, 1, ROW_BLOCK * TOP_K), m, p_rows)
        y = _ffn(block_e, n_used, xs, w_up, b_up.reshape(depth, N_EXPERTS, 1, -1), w_down,
                 b_down.reshape(depth, N_EXPERTS, 1, -1), n_blocks, l)
        dest4c = jnp.where(dest4 >= trash0, 0, dest4)
        if l < depth - 1:
            h = _combine(dest4c, gate, h1, row2(g_final), y, False, ROW_BLOCK, 0, rt // ROW_BLOCK)
    outs = []
    for seq0, n_seq, n_tok in layout["groups"]:
        tb = min(FINAL_BLOCK, n_tok)
        outs.append(_combine(dest4c, gate, h1, row2(g_final), y, True, tb, layout["tok_off"][seq0] // tb,
                             n_seq * n_tok // tb))
    return outs


def kernel(x_prompt, x_sample, meta_tokens, g_attn, w_in, g_q, w_uq, g_kv, w_ukv, rpb, g_out_na, g_out_mla, w_out,
           g_ffn, w_router, b_router, w_up, b_up, w_down, b_down, g_final):
    bp, lp, _ = x_prompt.shape
    bs, ls, _ = x_sample.shape
    seq_tokens = [lp] * bp + [ls] * bs
    layout = _make_layout(seq_tokens, [(0, bp, lp), (bp, bs, ls)])
    n_seq = len(seq_tokens)
    rt, n_valid = layout["rt"], layout["n_valid"]
    meta = jnp.broadcast_to(meta_tokens[None], (n_seq, N_META, D_MODEL)).reshape(n_seq * N_META, D_MODEL)
    h = jnp.concatenate([x_prompt.reshape(bp * lp, D_MODEL), x_sample.reshape(bs * ls, D_MODEL), meta,
                         jnp.zeros((rt - n_valid, D_MODEL), _F32)], axis=0)
    out = _forward(h, layout, meta_tokens, g_attn, w_in, g_q, w_uq, g_kv, w_ukv, rpb, g_out_na, g_out_mla, w_out,
                   g_ffn, w_router, b_router, w_up, b_up, w_down, b_down, g_final)
    return (out[0].reshape(bp, lp, D_MODEL), out[1].reshape(bs, ls, D_MODEL))
```

```python
import functools

import numpy as np
import jax
import jax.numpy as jnp
from jax import lax
from jax.experimental import pallas as pl
from jax.experimental.pallas import tpu as pltpu

D_MODEL = 1024
GRID_W = 64
N_META = 16
NA_HEADS = 8
NA_HEAD_DIM = 64
NA_WIN_H = 8
NA_WIN_W = 16
NA_WIDTH = NA_HEADS * NA_HEAD_DIM
MLA_HEADS = 8
QK_NOPE = 64
QK_ROPE = 32
V_HEAD = 64
Q_LORA = 256
KV_LORA = 128
ROPE_THETA = 10000.0
MLA_WIDTH = MLA_HEADS * V_HEAD
N_EXPERTS = 32
TOP_K = 4
D_FF = 1024
SWIGLU_LIMIT = 7.0
SWIGLU_ALPHA = 1.702
EPS = 1e-6

LANES = 128
HEAD_TILE = LANES
N_PAIRS = NA_HEADS // 2
ROW_BLOCK = 768
FINAL_BLOCK = 1024
NA_QROWS = 4
NA_QBLOCK = NA_QROWS * GRID_W
NA_KBLOCKS = 3
NA_STEP_PAIRS = 4
EXPERT_BLOCK = 512
ROW_TILE = (D_MODEL // LANES, LANES)
MLA_TQ = 1024
MLA_TK = 2048
MLA_CHUNK = 2048
MLA_STEP_PAIRS = 2
NEG_BIG = -1e30
LOG2_E = 1.4426950408889634
VMEM_LIMIT = 56 * 1024 * 1024

_F32 = jnp.float32
_BF16 = jnp.bfloat16


def _cparams(sem):
    return pltpu.CompilerParams(dimension_semantics=sem, vmem_limit_bytes=VMEM_LIMIT)


def _rms(x, g):
    return x * lax.rsqrt(jnp.mean(x * x, axis=-1, keepdims=True) + EPS) * g


def _dot(a, b):
    return jnp.dot(a, b, preferred_element_type=_F32)


def _dot_nt(a, b):
    return lax.dot_general(a, b, (((1,), (1,)), ((), ())), preferred_element_type=_F32)


def _attn_in_kernel(h_ref, cos_ref, sin_ref, g_attn_ref, w_in_ref, g_q_ref, w_q_ref, g_kv_ref,
                    w_kv_ref, w_kr_ref, vone_ref,
                    qna_ref, kna_ref, vna_ref, qm_ref, km_ref, vm_ref, kmt_ref):
    a = _rms(h_ref[...], g_attn_ref[...]).astype(_BF16)
    proj = _dot(a, w_in_ref[...])
    qna_ref[...] = proj[:, 0:NA_WIDTH].astype(_BF16)
    kna_ref[...] = proj[:, NA_WIDTH:2 * NA_WIDTH].astype(_BF16)
    vna_ref[...] = proj[:, 2 * NA_WIDTH:3 * NA_WIDTH].astype(_BF16)
    s2 = 3 * NA_WIDTH
    cq = proj[:, s2:s2 + Q_LORA]
    ckv = proj[:, s2 + Q_LORA:s2 + Q_LORA + KV_LORA]
    kr = proj[:, s2 + Q_LORA + KV_LORA:]
    cos = jnp.concatenate([cos_ref[...]] * MLA_HEADS, axis=1)
    sin = jnp.concatenate([sin_ref[...]] * MLA_HEADS, axis=1)
    width = MLA_HEADS * HEAD_TILE
    q2 = _dot(_rms(cq, g_q_ref[...]).astype(_BF16), w_q_ref[...])
    scale = (QK_NOPE + QK_ROPE) ** -0.5 * LOG2_E
    qm_ref[...] = ((q2[:, :width] * cos + q2[:, width:] * sin) * scale).astype(_BF16)
    kv2 = _dot(_rms(ckv, g_kv_ref[...]).astype(_BF16), w_kv_ref[...])
    kr2 = _dot(kr.astype(_BF16), w_kr_ref[...])
    km = (kv2[:, :width] + kr2[:, :width]) * cos + kr2[:, width:] * sin
    km_ref[...] = km.astype(_BF16)
    kmt_ref[...] = km.T.astype(_BF16)
    vm_ref[...] = (kv2[:, width:] + vone_ref[...]).astype(_BF16)


def _attn_in(h, cos_t, sin_t, g_attn, w_in_p, g_q, w_q_p, g_kv, w_kv_p, w_kr_p, vone):
    rt = h.shape[0]
    tb = ROW_BLOCK
    width = MLA_HEADS * HEAD_TILE
    row = lambda w: pl.BlockSpec((tb, w), lambda i: (i, 0))
    full = lambda a: pl.BlockSpec(a.shape, lambda i: (0,) * a.ndim)
    outs = ([jax.ShapeDtypeStruct((rt, NA_WIDTH), _BF16)] * 3 + [jax.ShapeDtypeStruct((rt, width), _BF16)] * 3
            + [jax.ShapeDtypeStruct((width, rt), _BF16)])
    return pl.pallas_call(
        _attn_in_kernel,
        grid=(rt // tb,),
        in_specs=[row(D_MODEL), row(LANES), row(LANES), full(g_attn), full(w_in_p), full(g_q), full(w_q_p),
                  full(g_kv), full(w_kv_p), full(w_kr_p), full(vone)],
        out_specs=[row(NA_WIDTH)] * 3 + [row(width)] * 3 + [pl.BlockSpec((width, tb), lambda i: (0, i))],
        out_shape=outs,
        compiler_params=_cparams(("parallel",)),
        name="attn_in",
    )(h, cos_t, sin_t, g_attn, w_in_p, g_q, w_q_p, g_kv, w_kv_p, w_kr_p, vone)


def _na_kernel(cidx_ref, var_ref, midx_ref, q_ref, kp_ref, kc_ref, kn_ref, vp_ref, vc_ref, vn_ref,
               km_ref, vm_ref, bias_ref, oin_ref, o_ref):
    del cidx_ref, var_ref, midx_ref, oin_ref
    lane = lax.broadcasted_iota(jnp.int32, (1, LANES), 1)
    for pair in range(q_ref.shape[1] // LANES):
        tile = slice(pair * LANES, (pair + 1) * LANES)
        q = q_ref[:, tile]
        ks = (kp_ref[:, tile], kc_ref[:, tile], kn_ref[:, tile])
        vs = (vp_ref[:, tile], vc_ref[:, tile], vn_ref[:, tile])
        km = km_ref[:, tile]
        vm = vm_ref[:, tile]
        outs = []
        for hh in range(2):
            in_head = (lane >= hh * NA_HEAD_DIM) & (lane < (hh + 1) * NA_HEAD_DIM)
            qh = jnp.where(in_head, q, jnp.zeros_like(q))
            s_loc = jnp.concatenate([_dot_nt(qh, k) for k in ks], axis=1) + bias_ref[0, 2 * pair + hh]
            s_met = _dot_nt(qh, km)
            m = jnp.maximum(jnp.max(s_loc, axis=-1, keepdims=True), jnp.max(s_met, axis=-1, keepdims=True))
            p_loc = jnp.exp(s_loc - m)
            p_met = jnp.exp(s_met - m)
            l = jnp.sum(p_loc, axis=-1, keepdims=True) + jnp.sum(p_met, axis=-1, keepdims=True)
            o = _dot(p_met.astype(_BF16), vm)
            for j in range(NA_KBLOCKS):
                o = o + _dot(p_loc[:, j * NA_QBLOCK:(j + 1) * NA_QBLOCK].astype(_BF16), vs[j])
            outs.append(o / l)
        o_ref[:, tile] = jnp.where(lane < NA_HEAD_DIM, outs[0], outs[1])


def _na_meta_kernel(midx_ref, q_ref, k_ref, v_ref, oin_ref, o_ref):
    del midx_ref, oin_ref
    q = q_ref[...]
    k = k_ref[...]
    v = v_ref[...]
    lane = lax.broadcasted_iota(jnp.int32, (1, LANES), 1)
    outs = []
    for hh in range(2):
        in_head = (lane >= hh * NA_HEAD_DIM) & (lane < (hh + 1) * NA_HEAD_DIM)
        s = _dot_nt(jnp.where(in_head, q, jnp.zeros_like(q)), k)
        p = jnp.exp(s - jnp.max(s, axis=-1, keepdims=True))
        outs.append(_dot(p.astype(_BF16), v) / jnp.sum(p, axis=-1, keepdims=True))
    o_ref[...] = jnp.where(lane < NA_HEAD_DIM, outs[0], outs[1])


def _na_bias(rpb_l):
    n_kr = NA_KBLOCKS * NA_QROWS
    qr = np.arange(NA_QROWS)[:, None]
    kr = np.arange(n_kr)[None, :]
    qc = np.arange(GRID_W)[:, None]
    kc = np.arange(GRID_W)[None, :]
    cs = np.clip(qc - NA_WIN_W // 2, 0, GRID_W - NA_WIN_W)
    col_ok = (kc >= cs) & (kc < cs + NA_WIN_W)
    dc = np.clip(kc - qc + NA_WIN_W - 1, 0, 2 * NA_WIN_W - 2)
    sel_c = (np.arange(2 * NA_WIN_W - 1)[None, None, :] == dc[:, :, None]) & col_ok[:, :, None]
    nrows = 4 * n_kr
    sel_r, ok = [], []
    for r0, k0 in ((0, 0), (NA_QROWS, 0), (nrows - NA_QROWS, nrows - n_kr)):
        r = r0 + qr
        key_row = k0 + kr
        rs = np.clip(r - NA_WIN_H // 2, 0, nrows - NA_WIN_H)
        row_ok = (key_row >= rs) & (key_row < rs + NA_WIN_H)
        dr = np.clip(key_row - r + NA_WIN_H - 1, 0, 2 * NA_WIN_H - 2)
        sel_r.append((np.arange(2 * NA_WIN_H - 1)[None, None, :] == dr[:, :, None]) & row_ok[:, :, None])
        ok.append(row_ok[:, None, :, None] & col_ok[None, :, None, :])
    sel_r = jnp.asarray(np.stack(sel_r), _F32)
    sel_c = jnp.asarray(sel_c, _F32)
    b = jnp.einsum("vqkd,hde,cxe->vhqckx", sel_r, rpb_l.astype(_F32), sel_c, precision=lax.Precision.HIGHEST)
    b = jnp.where(jnp.asarray(np.stack(ok))[:, None], b, NEG_BIG)
    return b.reshape(3, NA_HEADS, NA_QBLOCK, NA_KBLOCKS * NA_QBLOCK)


def _na_attention(q, k, v, bias, layout):
    rt = q.shape[0]
    cidx, var, midx = layout["na_cidx"], layout["na_var"], layout["na_midx"]
    nblk = cidx.shape[0]
    pw = NA_STEP_PAIRS * LANES
    qspec = pl.BlockSpec((NA_QBLOCK, pw), lambda p, b, c, vr, m: (b, p))
    kspec = lambda d: pl.BlockSpec((NA_QBLOCK, pw), lambda p, b, c, vr, m: (c[b] + d, p))
    mspec = pl.BlockSpec((N_META, pw), lambda p, b, c, vr, m: (m[b], p))
    bspec = pl.BlockSpec((1, 2 * NA_STEP_PAIRS, NA_QBLOCK, NA_KBLOCKS * NA_QBLOCK),
                         lambda p, b, c, vr, m: (vr[b], p, 0, 0))
    o = pl.pallas_call(
        _na_kernel,
        grid_spec=pltpu.PrefetchScalarGridSpec(
            num_scalar_prefetch=3,
            grid=(N_PAIRS // NA_STEP_PAIRS, nblk),
            in_specs=[qspec, kspec(-1), kspec(0), kspec(1), kspec(-1), kspec(0), kspec(1), mspec, mspec, bspec,
                      pl.BlockSpec(memory_space=pl.ANY)],
            out_specs=qspec),
        out_shape=jax.ShapeDtypeStruct((rt, NA_WIDTH), _F32),
        input_output_aliases={13: 0},
        compiler_params=_cparams(("parallel", "parallel")),
        name="na_attn",
    )(cidx, var, midx, q, k, k, k, v, v, v, k, v, bias, jnp.zeros((rt, NA_WIDTH), _F32))
    smidx = layout["seq_midx"]
    mq = pl.BlockSpec((N_META, LANES), lambda p, s, m: (m[s], p))
    return pl.pallas_call(
        _na_meta_kernel,
        grid_spec=pltpu.PrefetchScalarGridSpec(
            num_scalar_prefetch=1,
            grid=(N_PAIRS, smidx.shape[0]),
            in_specs=[mq, mq, mq, pl.BlockSpec(memory_space=pl.ANY)],
            out_specs=mq),
        out_shape=jax.ShapeDtypeStruct((rt, NA_WIDTH), _F32),
        input_output_aliases={4: 0},
        compiler_params=_cparams(("parallel", "parallel")),
        name="na_meta",
    )(smidx, q, k, v, o)


def _online_update(m_prev, acc, s, v):
    w = s.shape[1]
    if w % LANES == 0:
        parts = [s[:, j * LANES:(j + 1) * LANES] for j in range(w // LANES)]
        mx = parts[0]
        for part in parts[1:]:
            mx = jnp.maximum(mx, part)
        m_new = jnp.maximum(m_prev, jnp.max(mx, axis=-1, keepdims=True))
        p = jnp.concatenate([jnp.exp2(part - m_new) for part in parts], axis=1)
    else:
        m_new = jnp.maximum(m_prev, jnp.max(s, axis=-1, keepdims=True))
        p = jnp.exp2(s - m_new[:, :w])
    acc = jnp.exp2(m_prev - m_new) * acc + _dot(p.astype(_BF16), v)
    return m_new, acc


def _mla_kernel(q_ref, kt_ref, v_ref, km_ref, vm_ref, oin_ref, o_ref, m_sc, acc_sc):
    del oin_ref
    t = pl.program_id(3)
    nt = pl.num_programs(3)
    tk = v_ref.shape[0]
    chunk = min(MLA_CHUNK, tk)

    @pl.when(t == 0)
    def _():
        m_sc[...] = jnp.full(m_sc.shape, NEG_BIG, _F32)
        acc_sc[...] = jnp.zeros(acc_sc.shape, _F32)

    n_heads = m_sc.shape[0]
    for hh in range(n_heads):
        tile = slice(hh * HEAD_TILE, (hh + 1) * HEAD_TILE)
        q = q_ref[:, tile]
        m, acc = m_sc[hh], acc_sc[hh]
        for c in range(tk // chunk):
            cols = slice(c * chunk, (c + 1) * chunk)
            m, acc = _online_update(m, acc, _dot(q, kt_ref[tile, cols]), v_ref[cols, tile])
        m_sc[hh] = m
        acc_sc[hh] = acc

    @pl.when(t == nt - 1)
    def _():
        lane = lax.broadcasted_iota(jnp.int32, (1, LANES), 1)
        for pair in range(n_heads // 2):
            outs = []
            for hh in (2 * pair, 2 * pair + 1):
                tile = slice(hh * HEAD_TILE, (hh + 1) * HEAD_TILE)
                _, acc = _online_update(m_sc[hh], acc_sc[hh], _dot_nt(q_ref[:, tile], km_ref[:, tile]),
                                        vm_ref[:, tile])
                outs.append(acc)
            l0 = outs[0][:, V_HEAD:V_HEAD + 1]
            l1 = outs[1][:, 0:1]
            o_ref[:, pair * LANES:(pair + 1) * LANES] = jnp.where(lane < V_HEAD, outs[0] / l0, outs[1] / l1)


def _mla_call(q, kt, k, v, o_prev, *, tq, tk, n_seq, q_blk0, q_blk_stride, n_qblk, kv_blk0, kv_blk_stride, n_kvblk,
              meta_blk0, name):
    rt = q.shape[0]
    n_heads = 2 * MLA_STEP_PAIRS
    pw = n_heads * HEAD_TILE
    qspec = pl.BlockSpec((tq, pw), lambda s, p, i, t: (q_blk0 + s * q_blk_stride + i, p))
    ktspec = pl.BlockSpec((pw, tk), lambda s, p, i, t: (p, kv_blk0 + s * kv_blk_stride + t))
    vspec = pl.BlockSpec((tk, pw), lambda s, p, i, t: (kv_blk0 + s * kv_blk_stride + t, p))
    mspec = pl.BlockSpec((N_META, pw), lambda s, p, i, t: (meta_blk0 + s, p))
    ospec = pl.BlockSpec((tq, MLA_STEP_PAIRS * LANES), lambda s, p, i, t: (q_blk0 + s * q_blk_stride + i, p))
    return pl.pallas_call(
        _mla_kernel,
        grid=(n_seq, N_PAIRS // MLA_STEP_PAIRS, n_qblk, n_kvblk),
        in_specs=[qspec, ktspec, vspec, mspec, mspec, pl.BlockSpec(memory_space=pl.ANY)],
        out_specs=ospec,
        out_shape=jax.ShapeDtypeStruct((rt, MLA_WIDTH), _F32),
        scratch_shapes=[pltpu.VMEM((n_heads, tq, LANES), _F32), pltpu.VMEM((n_heads, tq, LANES), _F32)],
        input_output_aliases={5: 0},
        compiler_params=_cparams(("parallel", "parallel", "parallel", "arbitrary")),
        name=name,
    )(q, kt, v, k, v, o_prev)


def _mla_attention(q, kt, k, v, layout):
    rt = q.shape[0]
    o = jnp.zeros((rt, MLA_WIDTH), _F32)
    nt = layout["n_tok_total"]
    for gi, (seq0, n_seq, n_tok) in enumerate(layout["groups"]):
        tq = min(MLA_TQ, n_tok)
        tk = min(MLA_TK, n_tok)
        off = layout["tok_off"][seq0]
        assert off % tq == 0 and off % tk == 0 and n_tok % tq == 0 and n_tok % tk == 0
        o = _mla_call(q, kt, k, v, o, tq=tq, tk=tk, n_seq=n_seq, q_blk0=off // tq, q_blk_stride=n_tok // tq,
                      n_qblk=n_tok // tq, kv_blk0=off // tk, kv_blk_stride=n_tok // tk, n_kvblk=n_tok // tk,
                      meta_blk0=nt // N_META + seq0, name=f"mla_tok{gi}")
        o = _mla_call(q, kt, k, v, o, tq=N_META, tk=tk, n_seq=n_seq, q_blk0=nt // N_META + seq0, q_blk_stride=1,
                      n_qblk=1, kv_blk0=off // tk, kv_blk_stride=n_tok // tk, n_kvblk=n_tok // tk,
                      meta_blk0=nt // N_META + seq0, name=f"mla_meta{gi}")
    return o


def _attn_out_kernel(ona_ref, omla_ref, h_ref, g_na_ref, g_mla_ref, w_out_ref, g_ffn_ref, w_r_ref, b_r_ref,
                     h1_ref, m_ref, idx_ref, pos_ref, gate_ref, cnt_ref, cnt_sc, *, n_valid):
    i = pl.program_id(0)
    tb = h_ref.shape[0]

    @pl.when(i == 0)
    def _():
        cnt_sc[...] = jnp.zeros(cnt_sc.shape, _F32)

    n1 = _rms(ona_ref[...], g_na_ref[...]).astype(_BF16)
    n2 = _rms(omla_ref[...], g_mla_ref[...]).astype(_BF16)
    h1 = h_ref[...] + _dot(n1, w_out_ref[0:NA_WIDTH, :]) + _dot(n2, w_out_ref[NA_WIDTH:, :])
    h1_ref[...] = h1
    m = _rms(h1, g_ffn_ref[...])
    m_ref[...] = m.reshape(m_ref.shape)
    m_hi = m.astype(_BF16)
    m_lo = (m - m_hi.astype(_F32)).astype(_BF16)
    hi = _dot(m_hi, w_r_ref[...])
    logits = hi[:, :LANES] + (hi[:, LANES:] + _dot(m_lo, w_r_ref[:, :LANES])) + b_r_ref[...]

    lane = lax.broadcasted_iota(jnp.int32, (tb, LANES), 1).astype(_F32)
    row = lax.broadcasted_iota(jnp.int32, (tb, 1), 0) + i * tb
    valid = jnp.where(row < n_valid, 1.0, 0.0)
    work = logits
    sel = jnp.zeros((tb, LANES), _F32)
    idx_out = jnp.zeros((tb, LANES), _F32)
    top = []
    for kk in range(TOP_K):
        mx = jnp.max(work, axis=-1, keepdims=True)
        idx = jnp.min(jnp.where(work == mx, lane, float(LANES)), axis=-1, keepdims=True)
        hit = lane == idx
        sel = jnp.where(hit, 1.0, sel)
        work = jnp.where(hit, NEG_BIG * 2, work)
        idx_out = jnp.where(lane == kk, idx, idx_out)
        top.append((mx, idx))
    e = [jnp.exp(mx - top[0][0]) for mx, _ in top]
    denom = e[0] + e[1] + e[2] + e[3]
    gate_out = jnp.zeros((tb, LANES), _F32)
    for kk in range(TOP_K):
        gate_out = jnp.where(lane == kk, e[kk] / denom, gate_out)
    gate_ref[...] = gate_out * valid
    idx_ref[...] = idx_out

    sel = sel * valid
    r_i = lax.broadcasted_iota(jnp.int32, (tb, tb), 0)
    c_i = lax.broadcasted_iota(jnp.int32, (tb, tb), 1)
    tri = jnp.where(c_i < r_i, 1.0, 0.0).astype(_BF16)
    pos_full = _dot(tri, sel.astype(_BF16)) + cnt_sc[...]
    pos_out = jnp.zeros((tb, LANES), _F32)
    for kk in range(TOP_K):
        pk = jnp.sum(jnp.where(lane == top[kk][1], pos_full, 0.0), axis=-1, keepdims=True)
        pos_out = jnp.where(lane == kk, pk, pos_out)
    pos_ref[...] = pos_out
    cnt_sc[...] = cnt_sc[...] + jnp.sum(sel, axis=0, keepdims=True)
    cnt_ref[...] = cnt_sc[...]


def _attn_out(o_na, o_mla, h, g_na, g_mla, w_out, g_ffn, w_r, b_r, n_valid):
    rt = h.shape[0]
    tb = ROW_BLOCK
    row = lambda w: pl.BlockSpec((tb, w), lambda i: (i, 0))
    full = lambda a: pl.BlockSpec(a.shape, lambda i: (0,) * a.ndim)
    return pl.pallas_call(
        functools.partial(_attn_out_kernel, n_valid=n_valid),
        grid=(rt // tb,),
        in_specs=[row(NA_WIDTH), row(MLA_WIDTH), row(D_MODEL), full(g_na), full(g_mla), full(w_out), full(g_ffn),
                  full(w_r), full(b_r)],
        out_specs=[row(D_MODEL), pl.BlockSpec((tb,) + ROW_TILE, lambda i: (i, 0, 0)), row(LANES), row(LANES),
                   row(LANES), pl.BlockSpec((1, LANES), lambda i: (0, 0))],
        out_shape=[jax.ShapeDtypeStruct((rt, D_MODEL), _F32), jax.ShapeDtypeStruct((rt,) + ROW_TILE, _F32),
                   jax.ShapeDtypeStruct((rt, LANES), _F32), jax.ShapeDtypeStruct((rt, LANES), _F32),
                   jax.ShapeDtypeStruct((rt, LANES), _F32), jax.ShapeDtypeStruct((1, LANES), _F32)],
        scratch_shapes=[pltpu.VMEM((1, LANES), _F32)],
        compiler_params=_cparams(("arbitrary",)),
        name="attn_out_router",
    )(o_na, o_mla, h, g_na, g_mla, w_out, g_ffn, w_r, b_r)


def _dest_kernel(idx_ref, pos_ref, pstart_ref, dest_ref, *, n_valid, trash0):
    i = pl.program_id(0)
    tb = idx_ref.shape[0]
    lane_i = lax.broadcasted_iota(jnp.int32, (tb, LANES), 1)
    lane = lane_i.astype(_F32)
    row = lax.broadcasted_iota(jnp.int32, (tb, 1), 0) + i * tb
    valid = row < n_valid
    idx = idx_ref[...]
    out = pos_ref[...]
    for kk in range(TOP_K):
        start = jnp.sum(jnp.where(lane == idx[:, kk:kk + 1], pstart_ref[...], 0.0), axis=-1, keepdims=True)
        out = jnp.where(lane == kk, out + start, out)
    dest_ref[...] = jnp.where(valid, out.astype(jnp.int32), trash0 + (row - n_valid) * TOP_K + lane_i)


def _dest(idx, pos, pstart, n_valid, trash0):
    rt = idx.shape[0]
    tb = ROW_BLOCK
    row = pl.BlockSpec((tb, LANES), lambda i: (i, 0))
    return pl.pallas_call(
        functools.partial(_dest_kernel, n_valid=n_valid, trash0=trash0),
        grid=(rt // tb,),
        in_specs=[row, row, pl.BlockSpec((1, LANES), lambda i: (0, 0))],
        out_specs=row,
        out_shape=jax.ShapeDtypeStruct((rt, LANES), jnp.int32),
        compiler_params=_cparams(("parallel",)),
        name="moe_dest",
    )(idx, pos, pstart)


def _row_copies_wait(ref, n_rows, sem):
    pltpu.make_async_copy(ref.at[pl.ds(0, n_rows)], ref.at[pl.ds(0, n_rows)], sem).wait()


def _dispatch_kernel(free_ref, dest_ref, m_ref, xs_ref, zero_sc, sem_free, sem_rows):
    i = pl.program_id(0)
    tb = m_ref.shape[0]
    n_free = free_ref.shape[0]

    @pl.when(i == 0)
    def _():
        zero_sc[...] = jnp.zeros(zero_sc.shape, zero_sc.dtype)

        def fill(j, carry):
            for u in range(2):
                pltpu.make_async_copy(zero_sc, xs_ref.at[free_ref[2 * j + u]], sem_free).start(priority=u)
            return carry

        lax.fori_loop(0, n_free // 2, fill, 0)
        _row_copies_wait(xs_ref, n_free, sem_free)

    def body(t, carry):
        for kk in range(TOP_K):
            d = dest_ref[0, 0, t * TOP_K + kk]
            pltpu.make_async_copy(m_ref.at[t], xs_ref.at[d], sem_rows).start(priority=kk % 2)
        return carry

    lax.fori_loop(0, tb, body, 0)
    _row_copies_wait(xs_ref, tb * TOP_K, sem_rows)


def _dispatch(free_slots, dest3, m, p_rows):
    rt = m.shape[0]
    tb = ROW_BLOCK
    return pl.pallas_call(
        _dispatch_kernel,
        grid_spec=pltpu.PrefetchScalarGridSpec(
            num_scalar_prefetch=1,
            grid=(rt // tb,),
            in_specs=[pl.BlockSpec((1, 1, tb * TOP_K), lambda i, f: (i, 0, 0), memory_space=pltpu.SMEM),
                      pl.BlockSpec((tb,) + ROW_TILE, lambda i, f: (i, 0, 0))],
            out_specs=pl.BlockSpec(memory_space=pl.ANY),
            scratch_shapes=[pltpu.VMEM(ROW_TILE, _F32), pltpu.SemaphoreType.DMA(()),
                            pltpu.SemaphoreType.DMA(())]),
        out_shape=jax.ShapeDtypeStruct((p_rows,) + ROW_TILE, _F32),
        compiler_params=_cparams(("arbitrary",)),
        name="moe_dispatch",
    )(free_slots, dest3, m)


def _ffn_kernel(be_ref, nu_ref, x_ref, wu_ref, bu_ref, wd_ref, bd_ref, y_ref, wu_sc, wd_sc):
    i = pl.program_id(0)

    @pl.when(i < nu_ref[0])
    def _():
        prev = be_ref[jnp.maximum(i - 1, 0)]

        @pl.when((i == 0) | (be_ref[i] != prev))
        def _():
            wu_sc[...] = wu_ref[0, 0].astype(_BF16)
            wd_sc[...] = wd_ref[0, 0].astype(_BF16)

        x = x_ref[...].reshape(x_ref.shape[0], D_MODEL)
        h = _dot(x.astype(_BF16), wu_sc[...]) + bu_ref[0, 0]
        gate = jnp.minimum(h[:, :D_FF], SWIGLU_LIMIT)
        up = jnp.clip(h[:, D_FF:], -SWIGLU_LIMIT, SWIGLU_LIMIT)
        glu = gate * jax.nn.sigmoid(gate * SWIGLU_ALPHA)
        y = _dot(((up + 1.0) * glu).astype(_BF16), wd_sc[...]) + bd_ref[0, 0]
        y_ref[...] = y.reshape(y_ref.shape)

    @pl.when(i >= nu_ref[0])
    def _():
        y_ref[...] = jnp.zeros(y_ref.shape, y_ref.dtype)


def _ffn(block_e, n_used, xs, w_up, b_up, w_down, b_down, n_blocks, layer):
    bm = EXPERT_BLOCK
    return pl.pallas_call(
        _ffn_kernel,
        grid_spec=pltpu.PrefetchScalarGridSpec(
            num_scalar_prefetch=2,
            grid=(n_blocks,),
            in_specs=[pl.BlockSpec((bm,) + ROW_TILE, lambda i, be, nu: (i, 0, 0)),
                      pl.BlockSpec((1, 1, D_MODEL, 2 * D_FF), lambda i, be, nu: (layer, be[i], 0, 0)),
                      pl.BlockSpec((1, 1, 1, 2 * D_FF), lambda i, be, nu: (layer, be[i], 0, 0)),
                      pl.BlockSpec((1, 1, D_FF, D_MODEL), lambda i, be, nu: (layer, be[i], 0, 0)),
                      pl.BlockSpec((1, 1, 1, D_MODEL), lambda i, be, nu: (layer, be[i], 0, 0))],
            out_specs=pl.BlockSpec((bm,) + ROW_TILE, lambda i, be, nu: (i, 0, 0)),
            scratch_shapes=[pltpu.VMEM((D_MODEL, 2 * D_FF), _BF16), pltpu.VMEM((D_FF, D_MODEL), _BF16)]),
        out_shape=jax.ShapeDtypeStruct((n_blocks * bm,) + ROW_TILE, _F32),
        compiler_params=_cparams(("arbitrary",)),
        name="moe_ffn",
    )(block_e, n_used, xs, w_up, b_up, w_down, b_down)


def _combine_kernel(dest_ref, gate_ref, h1_ref, g_ref, y_ref, o_ref, buf_sc, sem, *, final):
    tb = h1_ref.shape[0]

    def body(t, carry):
        for kk in range(TOP_K):
            d = dest_ref[0, 0, t * TOP_K + kk]
            pltpu.make_async_copy(y_ref.at[d], buf_sc.at[kk, t], sem).start(priority=kk % 2)
        return carry

    lax.fori_loop(0, tb, body, 0)
    _row_copies_wait(y_ref, tb * TOP_K, sem)
    gate = gate_ref[...]
    out = h1_ref[...]
    for kk in range(TOP_K):
        out = out + gate[:, kk:kk + 1] * buf_sc[kk].reshape(tb, D_MODEL)
    if final:
        out = _rms(out, g_ref[...])
    o_ref[...] = out


def _combine(dest4, gate, h1, g_final, y, final, tb, blk0, nblk):
    dest3 = dest4[blk0 * tb:(blk0 + nblk) * tb].reshape(nblk, 1, tb * TOP_K)
    return pl.pallas_call(
        functools.partial(_combine_kernel, final=final),
        grid=(nblk,),
        in_specs=[pl.BlockSpec((1, 1, tb * TOP_K), lambda i: (i, 0, 0), memory_space=pltpu.SMEM),
                  pl.BlockSpec((tb, LANES), lambda i: (blk0 + i, 0)),
                  pl.BlockSpec((tb, D_MODEL), lambda i: (blk0 + i, 0)),
                  pl.BlockSpec((1, D_MODEL), lambda i: (0, 0)),
                  pl.BlockSpec(memory_space=pl.ANY)],
        out_specs=pl.BlockSpec((tb, D_MODEL), lambda i: (i, 0)),
        out_shape=jax.ShapeDtypeStruct((nblk * tb, D_MODEL), _F32),
        scratch_shapes=[pltpu.VMEM((TOP_K, tb) + ROW_TILE, _F32), pltpu.SemaphoreType.DMA(())],
        compiler_params=_cparams(("arbitrary",)),
        name="moe_combine",
    )(dest3, gate, h1, g_final, y)


def _make_layout(seq_tokens, groups):
    n_seq = len(seq_tokens)
    tok_off = np.concatenate([[0], np.cumsum(seq_tokens)]).astype(np.int64)
    nt = int(tok_off[-1])
    n_valid = nt + n_seq * N_META
    rt = -(-n_valid // ROW_BLOCK) * ROW_BLOCK
    pos = np.zeros((rt,), np.float32)
    cidx, var, midx = [], [], []
    for s, n in enumerate(seq_tokens):
        assert n % NA_QBLOCK == 0 and n // NA_QBLOCK >= NA_KBLOCKS
        pos[tok_off[s]:tok_off[s] + n] = N_META + np.arange(n)
        pos[nt + s * N_META:nt + (s + 1) * N_META] = np.arange(N_META)
        nb = n // NA_QBLOCK
        b0 = int(tok_off[s]) // NA_QBLOCK
        for b in range(nb):
            cidx.append(b0 + min(max(b, 1), nb - 2))
            var.append(0 if b == 0 else (2 if b == nb - 1 else 1))
            midx.append(nt // N_META + s)
    return {
        "seq_tokens": tuple(seq_tokens), "groups": tuple(groups), "tok_off": tuple(int(v) for v in tok_off),
        "n_tok_total": nt, "n_valid": n_valid, "rt": rt, "pos": pos,
        "na_cidx": jnp.asarray(cidx, jnp.int32), "na_var": jnp.asarray(var, jnp.int32),
        "na_midx": jnp.asarray(midx, jnp.int32),
        "seq_midx": jnp.asarray([nt // N_META + s for s in range(n_seq)], jnp.int32),
    }


def _rope_tables(pos):
    freqs = jnp.power(ROPE_THETA, -jnp.arange(0, QK_ROPE, 2, dtype=_F32) / QK_ROPE)
    ang = jnp.asarray(pos)[:, None] * freqs[None, :]
    cos, sin = jnp.cos(ang), jnp.sin(ang)
    rt = pos.shape[0]
    pad = LANES - QK_NOPE - QK_ROPE
    cos_t = jnp.concatenate([jnp.ones((rt, QK_NOPE), _F32), cos, cos, jnp.zeros((rt, pad), _F32)], axis=1)
    sin_t = jnp.concatenate([jnp.zeros((rt, QK_NOPE), _F32), sin, sin, jnp.zeros((rt, pad), _F32)], axis=1)
    return cos_t, sin_t


def _layer_weights(w_in, w_uq, w_ukv):
    half = QK_ROPE // 2
    s2 = 3 * NA_WIDTH
    kr_cols = w_in[:, s2 + Q_LORA + KV_LORA:]
    w_in_p = jnp.concatenate([w_in[:, :NA_WIDTH] * (NA_HEAD_DIM ** -0.5), w_in[:, NA_WIDTH:s2 + Q_LORA + KV_LORA],
                              kr_cols, jnp.zeros((D_MODEL, LANES - QK_ROPE), _F32)], axis=1).astype(_BF16)
    dq = QK_NOPE + QK_ROPE
    wq = w_uq.reshape(Q_LORA, MLA_HEADS, dq)
    zq = jnp.zeros((Q_LORA, MLA_HEADS, LANES - dq), _F32)
    q_plain = jnp.concatenate([wq, zq], axis=2)
    q_rot = jnp.concatenate([jnp.zeros((Q_LORA, MLA_HEADS, QK_NOPE), _F32), -wq[:, :, QK_NOPE + half:],
                             wq[:, :, QK_NOPE:QK_NOPE + half], zq], axis=2)
    w_q_p = jnp.concatenate([q_plain.reshape(Q_LORA, -1), q_rot.reshape(Q_LORA, -1)], axis=1).astype(_BF16)
    wkv = w_ukv.reshape(KV_LORA, MLA_HEADS, QK_NOPE + V_HEAD)
    k_plain = jnp.concatenate([wkv[:, :, :QK_NOPE], jnp.zeros((KV_LORA, MLA_HEADS, LANES - QK_NOPE), _F32)], axis=2)
    wv = wkv[:, :, QK_NOPE:].reshape(KV_LORA, N_PAIRS, 2, V_HEAD)
    zv = jnp.zeros((KV_LORA, N_PAIRS, LANES - V_HEAD), _F32)
    v_even = jnp.concatenate([wv[:, :, 0], zv], axis=2)
    v_odd = jnp.concatenate([zv, wv[:, :, 1]], axis=2)
    v_plain = jnp.stack([v_even, v_odd], axis=2)
    w_kv_p = jnp.concatenate([k_plain.reshape(KV_LORA, -1), v_plain.reshape(KV_LORA, -1)], axis=1).astype(_BF16)
    return w_in_p, w_q_p, w_kv_p


def _const_tables():
    half = QK_ROPE // 2
    width = MLA_HEADS * HEAD_TILE
    r_plain = np.zeros((LANES, width), np.float32)
    r_rot = np.zeros((LANES, width), np.float32)
    vone = np.zeros((1, width), np.float32)
    for h in range(MLA_HEADS):
        base = h * HEAD_TILE + QK_NOPE
        for j in range(QK_ROPE):
            r_plain[j, base + j] = 1.0
        for j in range(half):
            r_rot[half + j, base + j] = -1.0
            r_rot[j, base + half + j] = 1.0
        vone[0, h * HEAD_TILE + (V_HEAD if h % 2 == 0 else 0)] = 1.0
    w_kr_p = jnp.asarray(np.concatenate([r_plain, r_rot], axis=1), _BF16)
    return w_kr_p, jnp.asarray(vone)


def _moe_plan(counts, n_blocks, n_free):
    counts = counts[0, :N_EXPERTS].astype(jnp.int32)
    padded = (counts + EXPERT_BLOCK - 1) // EXPERT_BLOCK * EXPERT_BLOCK
    pend = jnp.cumsum(padded)
    pstart = pend - padded
    first_row = jnp.arange(n_blocks, dtype=jnp.int32) * EXPERT_BLOCK
    block_e = jnp.minimum(jnp.sum((pend[None, :] <= first_row[:, None]).astype(jnp.int32), axis=1), N_EXPERTS - 1)
    n_used = (pend[-1:] // EXPERT_BLOCK).astype(jnp.int32)
    pstart_row = jnp.zeros((1, LANES), _F32).at[0, :N_EXPERTS].set(pstart.astype(_F32))
    gap = jnp.concatenate([padded - counts, (n_blocks * EXPERT_BLOCK - pend[-1:])])
    gap_end = jnp.cumsum(gap)
    gap_first = jnp.concatenate([pstart + counts, pend[-1:]])
    j = jnp.arange(n_free, dtype=jnp.int32)
    seg = jnp.sum((gap_end[None, :] <= j[:, None]).astype(jnp.int32), axis=1)
    free_slots = (gap_first[seg] + j - (gap_end - gap)[seg]).astype(jnp.int32)
    return pstart_row, block_e, n_used, free_slots


def _forward(h, layout, meta_tokens, g_attn, w_in, g_q, w_uq, g_kv, w_ukv, rpb, g_out_na, g_out_mla, w_out,
             g_ffn, w_router, b_router, w_up, b_up, w_down, b_down, g_final):
    del meta_tokens
    depth = w_in.shape[0]
    rt = layout["rt"]
    n_valid = layout["n_valid"]
    cos_t, sin_t = _rope_tables(layout["pos"])
    w_kr_p, vone = _const_tables()
    n_assign = n_valid * TOP_K
    n_blocks = -(-n_assign // EXPERT_BLOCK) + N_EXPERTS
    trash0 = n_blocks * EXPERT_BLOCK
    p_rows = trash0 + (rt - n_valid) * TOP_K
    n_free = trash0 - n_assign
    row2 = lambda a: a.reshape(1, -1)
    for l in range(depth):
        w_in_p, w_q_p, w_kv_p = _layer_weights(w_in[l], w_uq[l], w_ukv[l])
        qna, kna, vna, qm, km, vm, kmt = _attn_in(h, cos_t, sin_t, row2(g_attn[l]), w_in_p, row2(g_q[l]), w_q_p,
                                             row2(g_kv[l]), w_kv_p, w_kr_p, vone)
        o_na = _na_attention(qna, kna, vna, _na_bias(rpb[l]), layout)
        o_mla = _mla_attention(qm, kmt, km, vm, layout)
        w_r32 = jnp.concatenate([w_router[l], jnp.zeros((D_MODEL, LANES - N_EXPERTS), _F32)], axis=1)
        w_r_hi = w_r32.astype(_BF16)
        w_r = jnp.concatenate([w_r_hi, (w_r32 - w_r_hi.astype(_F32)).astype(_BF16)], axis=1)
        b_r = jnp.concatenate([b_router[l], jnp.full((LANES - N_EXPERTS,), NEG_BIG, _F32)]).reshape(1, LANES)
        h1, m, idx, pos, gate, counts = _attn_out(o_na, o_mla, h, row2(g_out_na[l]), row2(g_out_mla[l]),
                                                  w_out[l].astype(_BF16), row2(g_ffn[l]), w_r, b_r, n_valid)
        pstart_row, block_e, n_used, free_slots = _moe_plan(counts, n_blocks, n_free)
        dest = _dest(idx, pos, pstart_row, n_valid, trash0)
        dest4 = dest[:, :TOP_K]
        xs = _dispatch(free_slots, dest4.reshape(rt // ROW_BLOCK, 1, ROW_BLOCK * TOP_K), m, p_rows)
        y = _ffn(block_e, n_used, xs, w_up, b_up.reshape(depth, N_EXPERTS, 1, -1), w_down,
                 b_down.reshape(depth, N_EXPERTS, 1, -1), n_blocks, l)
        dest4c = jnp.where(dest4 >= trash0, 0, dest4)
        if l < depth - 1:
            h = _combine(dest4c, gate, h1, row2(g_final), y, False, ROW_BLOCK, 0, rt // ROW_BLOCK)
    outs = []
    for seq0, n_seq, n_tok in layout["groups"]:
        tb = min(FINAL_BLOCK, n_tok)
        outs.append(_combine(dest4c, gate, h1, row2(g_final), y, True, tb, layout["tok_off"][seq0] // tb,
                             n_seq * n_tok // tb))
    return outs


def kernel(x_prompt, x_sample, meta_tokens, g_attn, w_in, g_q, w_uq, g_kv, w_ukv, rpb, g_out_na, g_out_mla, w_out,
           g_ffn, w_router, b_router, w_up, b_up, w_down, b_down, g_final):
    bp, lp, _ = x_prompt.shape
    bs, ls, _ = x_sample.shape
    seq_tokens = [lp] * bp + [ls] * bs
    layout = _make_layout(seq_tokens, [(0, bp, lp), (bp, bs, ls)])
    n_seq = len(seq_tokens)
    rt, n_valid = layout["rt"], layout["n_valid"]
    meta = jnp.broadcast_to(meta_tokens[None], (n_seq, N_META, D_MODEL)).reshape(n_seq * N_META, D_MODEL)
    h = jnp.concatenate([x_prompt.reshape(bp * lp, D_MODEL), x_sample.reshape(bs * ls, D_MODEL), meta,
                         jnp.zeros((rt - n_valid, D_MODEL), _F32)], axis=0)
    out = _forward(h, layout, meta_tokens, g_attn, w_in, g_q, w_uq, g_kv, w_ukv, rpb, g_out_na, g_out_mla, w_out,
                   g_ffn, w_router, b_router, w_up, b_up, w_down, b_down, g_final)
    return (out[0].reshape(bp, lp, D_MODEL), out[1].reshape(bs, ls, D_MODEL))
```

```python
import functools

import numpy as np
import jax
import jax.numpy as jnp
from jax import lax
from jax.experimental import pallas as pl
from jax.experimental.pallas import tpu as pltpu

D_MODEL = 1024
GRID_W = 64
N_META = 16
NA_HEADS = 8
NA_HEAD_DIM = 64
NA_WIN_H = 8
NA_WIN_W = 16
NA_WIDTH = NA_HEADS * NA_HEAD_DIM
MLA_HEADS = 8
QK_NOPE = 64
QK_ROPE = 32
V_HEAD = 64
Q_LORA = 256
KV_LORA = 128
ROPE_THETA = 10000.0
MLA_WIDTH = MLA_HEADS * V_HEAD
N_EXPERTS = 32
TOP_K = 4
D_FF = 1024
SWIGLU_LIMIT = 7.0
SWIGLU_ALPHA = 1.702
EPS = 1e-6

LANES = 128
HEAD_TILE = LANES
N_PAIRS = NA_HEADS // 2
ROW_BLOCK = 768
FINAL_BLOCK = 1024
NA_QROWS = 4
NA_QBLOCK = NA_QROWS * GRID_W
NA_KBLOCKS = 3
NA_STEP_PAIRS = 4
EXPERT_BLOCK = 512
ROW_TILE = (D_MODEL // LANES, LANES)
MLA_TQ = 1024
MLA_TK = 2048
MLA_CHUNK = 2048
MLA_STEP_PAIRS = 2
NEG_BIG = -1e30
LOG2_E = 1.4426950408889634
VMEM_LIMIT = 56 * 1024 * 1024

_F32 = jnp.float32
_BF16 = jnp.bfloat16


def _cparams(sem):
    return pltpu.CompilerParams(dimension_semantics=sem, vmem_limit_bytes=VMEM_LIMIT)


def _rms(x, g):
    return x * lax.rsqrt(jnp.mean(x * x, axis=-1, keepdims=True) + EPS) * g


def _dot(a, b):
    return jnp.dot(a, b, preferred_element_type=_F32)


def _dot_nt(a, b):
    return lax.dot_general(a, b, (((1,), (1,)), ((), ())), preferred_element_type=_F32)


def _attn_in_kernel(h_ref, cos_ref, sin_ref, g_attn_ref, w_in_ref, g_q_ref, w_q_ref, g_kv_ref,
                    w_kv_ref, w_kr_ref, vone_ref,
                    qna_ref, kna_ref, vna_ref, qm_ref, km_ref, vm_ref, kmt_ref):
    a = _rms(h_ref[...], g_attn_ref[...]).astype(_BF16)
    proj = _dot(a, w_in_ref[...])
    qna_ref[...] = (proj[:, 0:NA_WIDTH] * LOG2_E).astype(_BF16)
    kna_ref[...] = proj[:, NA_WIDTH:2 * NA_WIDTH].astype(_BF16)
    vna_ref[...] = proj[:, 2 * NA_WIDTH:3 * NA_WIDTH].astype(_BF16)
    s2 = 3 * NA_WIDTH
    cq = proj[:, s2:s2 + Q_LORA]
    ckv = proj[:, s2 + Q_LORA:s2 + Q_LORA + KV_LORA]
    kr = proj[:, s2 + Q_LORA + KV_LORA:]
    cos = jnp.concatenate([cos_ref[...]] * MLA_HEADS, axis=1)
    sin = jnp.concatenate([sin_ref[...]] * MLA_HEADS, axis=1)
    width = MLA_HEADS * HEAD_TILE
    q2 = _dot(_rms(cq, g_q_ref[...]).astype(_BF16), w_q_ref[...])
    scale = (QK_NOPE + QK_ROPE) ** -0.5 * LOG2_E
    qm_ref[...] = ((q2[:, :width] * cos + q2[:, width:] * sin) * scale).astype(_BF16)
    kv2 = _dot(_rms(ckv, g_kv_ref[...]).astype(_BF16), w_kv_ref[...])
    kr2 = _dot(kr.astype(_BF16), w_kr_ref[...])
    km = (kv2[:, :width] + kr2[:, :width]) * cos + kr2[:, width:] * sin
    km_ref[...] = km.astype(_BF16)
    kmt_ref[...] = km.T.astype(_BF16)
    vm_ref[...] = (kv2[:, width:] + vone_ref[...]).astype(_BF16)


def _attn_in(h, cos_t, sin_t, g_attn, w_in_p, g_q, w_q_p, g_kv, w_kv_p, w_kr_p, vone):
    rt = h.shape[0]
    tb = ROW_BLOCK
    width = MLA_HEADS * HEAD_TILE
    row = lambda w: pl.BlockSpec((tb, w), lambda i: (i, 0))
    full = lambda a: pl.BlockSpec(a.shape, lambda i: (0,) * a.ndim)
    outs = ([jax.ShapeDtypeStruct((rt, NA_WIDTH), _BF16)] * 3 + [jax.ShapeDtypeStruct((rt, width), _BF16)] * 3
            + [jax.ShapeDtypeStruct((width, rt), _BF16)])
    return pl.pallas_call(
        _attn_in_kernel,
        grid=(rt // tb,),
        in_specs=[row(D_MODEL), row(LANES), row(LANES), full(g_attn), full(w_in_p), full(g_q), full(w_q_p),
                  full(g_kv), full(w_kv_p), full(w_kr_p), full(vone)],
        out_specs=[row(NA_WIDTH)] * 3 + [row(width)] * 3 + [pl.BlockSpec((width, tb), lambda i: (0, i))],
        out_shape=outs,
        compiler_params=_cparams(("parallel",)),
        name="attn_in",
    )(h, cos_t, sin_t, g_attn, w_in_p, g_q, w_q_p, g_kv, w_kv_p, w_kr_p, vone)


def _na_kernel(cidx_ref, var_ref, midx_ref, q_ref, kp_ref, kc_ref, kn_ref, vp_ref, vc_ref, vn_ref,
               km_ref, vm_ref, bias_ref, oin_ref, o_ref):
    del cidx_ref, var_ref, midx_ref, oin_ref
    lane = lax.broadcasted_iota(jnp.int32, (1, LANES), 1)
    for pair in range(q_ref.shape[1] // LANES):
        tile = slice(pair * LANES, (pair + 1) * LANES)
        q = q_ref[:, tile]
        ks = (kp_ref[:, tile], kc_ref[:, tile], kn_ref[:, tile])
        vs = (vp_ref[:, tile], vc_ref[:, tile], vn_ref[:, tile])
        km = km_ref[:, tile]
        vm = vm_ref[:, tile]
        outs = []
        for hh in range(2):
            in_head = (lane >= hh * NA_HEAD_DIM) & (lane < (hh + 1) * NA_HEAD_DIM)
            qh = jnp.where(in_head, q, jnp.zeros_like(q))
            s_loc = jnp.concatenate([_dot_nt(qh, k) for k in ks], axis=1) + bias_ref[0, 2 * pair + hh]
            s_met = _dot_nt(qh, km)
            m = jnp.maximum(jnp.max(s_loc, axis=-1, keepdims=True), jnp.max(s_met, axis=-1, keepdims=True))
            p_loc = jnp.exp2(s_loc - m)
            p_met = jnp.exp2(s_met - m)
            l = jnp.sum(p_loc, axis=-1, keepdims=True) + jnp.sum(p_met, axis=-1, keepdims=True)
            o = _dot(p_met.astype(_BF16), vm)
            for j in range(NA_KBLOCKS):
                o = o + _dot(p_loc[:, j * NA_QBLOCK:(j + 1) * NA_QBLOCK].astype(_BF16), vs[j])
            outs.append(o / l)
        o_ref[:, tile] = jnp.where(lane < NA_HEAD_DIM, outs[0], outs[1])


def _na_meta_kernel(midx_ref, q_ref, k_ref, v_ref, oin_ref, o_ref):
    del midx_ref, oin_ref
    q = q_ref[...]
    k = k_ref[...]
    v = v_ref[...]
    lane = lax.broadcasted_iota(jnp.int32, (1, LANES), 1)
    outs = []
    for hh in range(2):
        in_head = (lane >= hh * NA_HEAD_DIM) & (lane < (hh + 1) * NA_HEAD_DIM)
        s = _dot_nt(jnp.where(in_head, q, jnp.zeros_like(q)), k)
        p = jnp.exp2(s - jnp.max(s, axis=-1, keepdims=True))
        outs.append(_dot(p.astype(_BF16), v) / jnp.sum(p, axis=-1, keepdims=True))
    o_ref[...] = jnp.where(lane < NA_HEAD_DIM, outs[0], outs[1])


def _na_bias(rpb_l):
    n_kr = NA_KBLOCKS * NA_QROWS
    qr = np.arange(NA_QROWS)[:, None]
    kr = np.arange(n_kr)[None, :]
    qc = np.arange(GRID_W)[:, None]
    kc = np.arange(GRID_W)[None, :]
    cs = np.clip(qc - NA_WIN_W // 2, 0, GRID_W - NA_WIN_W)
    col_ok = (kc >= cs) & (kc < cs + NA_WIN_W)
    dc = np.clip(kc - qc + NA_WIN_W - 1, 0, 2 * NA_WIN_W - 2)
    sel_c = (np.arange(2 * NA_WIN_W - 1)[None, None, :] == dc[:, :, None]) & col_ok[:, :, None]
    nrows = 4 * n_kr
    sel_r, ok = [], []
    for r0, k0 in ((0, 0), (NA_QROWS, 0), (nrows - NA_QROWS, nrows - n_kr)):
        r = r0 + qr
        key_row = k0 + kr
        rs = np.clip(r - NA_WIN_H // 2, 0, nrows - NA_WIN_H)
        row_ok = (key_row >= rs) & (key_row < rs + NA_WIN_H)
        dr = np.clip(key_row - r + NA_WIN_H - 1, 0, 2 * NA_WIN_H - 2)
        sel_r.append((np.arange(2 * NA_WIN_H - 1)[None, None, :] == dr[:, :, None]) & row_ok[:, :, None])
        ok.append(row_ok[:, None, :, None] & col_ok[None, :, None, :])
    sel_r = jnp.asarray(np.stack(sel_r), _F32)
    sel_c = jnp.asarray(sel_c, _F32)
    b = jnp.einsum("vqkd,hde,cxe->vhqckx", sel_r, rpb_l.astype(_F32), sel_c, precision=lax.Precision.HIGHEST)
    b = jnp.where(jnp.asarray(np.stack(ok))[:, None], b * LOG2_E, NEG_BIG)
    return b.reshape(3, NA_HEADS, NA_QBLOCK, NA_KBLOCKS * NA_QBLOCK)


def _na_attention(q, k, v, bias, layout):
    rt = q.shape[0]
    cidx, var, midx = layout["na_cidx"], layout["na_var"], layout["na_midx"]
    nblk = cidx.shape[0]
    pw = NA_STEP_PAIRS * LANES
    qspec = pl.BlockSpec((NA_QBLOCK, pw), lambda p, b, c, vr, m: (b, p))
    kspec = lambda d: pl.BlockSpec((NA_QBLOCK, pw), lambda p, b, c, vr, m: (c[b] + d, p))
    mspec = pl.BlockSpec((N_META, pw), lambda p, b, c, vr, m: (m[b], p))
    bspec = pl.BlockSpec((1, 2 * NA_STEP_PAIRS, NA_QBLOCK, NA_KBLOCKS * NA_QBLOCK),
                         lambda p, b, c, vr, m: (vr[b], p, 0, 0))
    o = pl.pallas_call(
        _na_kernel,
        grid_spec=pltpu.PrefetchScalarGridSpec(
            num_scalar_prefetch=3,
            grid=(N_PAIRS // NA_STEP_PAIRS, nblk),
            in_specs=[qspec, kspec(-1), kspec(0), kspec(1), kspec(-1), kspec(0), kspec(1), mspec, mspec, bspec,
                      pl.BlockSpec(memory_space=pl.ANY)],
            out_specs=qspec),
        out_shape=jax.ShapeDtypeStruct((rt, NA_WIDTH), _F32),
        input_output_aliases={13: 0},
        compiler_params=_cparams(("parallel", "parallel")),
        name="na_attn",
    )(cidx, var, midx, q, k, k, k, v, v, v, k, v, bias, jnp.zeros((rt, NA_WIDTH), _F32))
    smidx = layout["seq_midx"]
    mq = pl.BlockSpec((N_META, LANES), lambda p, s, m: (m[s], p))
    return pl.pallas_call(
        _na_meta_kernel,
        grid_spec=pltpu.PrefetchScalarGridSpec(
            num_scalar_prefetch=1,
            grid=(N_PAIRS, smidx.shape[0]),
            in_specs=[mq, mq, mq, pl.BlockSpec(memory_space=pl.ANY)],
            out_specs=mq),
        out_shape=jax.ShapeDtypeStruct((rt, NA_WIDTH), _F32),
        input_output_aliases={4: 0},
        compiler_params=_cparams(("parallel", "parallel")),
        name="na_meta",
    )(smidx, q, k, v, o)


def _online_update(m_prev, acc, s, v):
    w = s.shape[1]
    if w % LANES == 0:
        parts = [s[:, j * LANES:(j + 1) * LANES] for j in range(w // LANES)]
        mx = parts[0]
        for part in parts[1:]:
            mx = jnp.maximum(mx, part)
        m_new = jnp.maximum(m_prev, jnp.max(mx, axis=-1, keepdims=True))
        p = jnp.concatenate([jnp.exp2(part - m_new) for part in parts], axis=1)
    else:
        m_new = jnp.maximum(m_prev, jnp.max(s, axis=-1, keepdims=True))
        p = jnp.exp2(s - m_new[:, :w])
    acc = jnp.exp2(m_prev - m_new) * acc + _dot(p.astype(_BF16), v)
    return m_new, acc


def _mla_kernel(q_ref, kt_ref, v_ref, km_ref, vm_ref, oin_ref, o_ref, m_sc, acc_sc):
    del oin_ref
    t = pl.program_id(3)
    nt = pl.num_programs(3)
    tk = v_ref.shape[0]
    chunk = min(MLA_CHUNK, tk)

    @pl.when(t == 0)
    def _():
        m_sc[...] = jnp.full(m_sc.shape, NEG_BIG, _F32)
        acc_sc[...] = jnp.zeros(acc_sc.shape, _F32)

    n_heads = m_sc.shape[0]
    for hh in range(n_heads):
        tile = slice(hh * HEAD_TILE, (hh + 1) * HEAD_TILE)
        q = q_ref[:, tile]
        m, acc = m_sc[hh], acc_sc[hh]
        for c in range(tk // chunk):
            cols = slice(c * chunk, (c + 1) * chunk)
            m, acc = _online_update(m, acc, _dot(q, kt_ref[tile, cols]), v_ref[cols, tile])
        m_sc[hh] = m
        acc_sc[hh] = acc

    @pl.when(t == nt - 1)
    def _():
        lane = lax.broadcasted_iota(jnp.int32, (1, LANES), 1)
        for pair in range(n_heads // 2):
            outs = []
            for hh in (2 * pair, 2 * pair + 1):
                tile = slice(hh * HEAD_TILE, (hh + 1) * HEAD_TILE)
                _, acc = _online_update(m_sc[hh], acc_sc[hh], _dot_nt(q_ref[:, tile], km_ref[:, tile]),
                                        vm_ref[:, tile])
                outs.append(acc)
            l0 = outs[0][:, V_HEAD:V_HEAD + 1]
            l1 = outs[1][:, 0:1]
            o_ref[:, pair * LANES:(pair + 1) * LANES] = jnp.where(lane < V_HEAD, outs[0] / l0, outs[1] / l1)


def _mla_call(q, kt, k, v, o_prev, *, tq, tk, n_seq, q_blk0, q_blk_stride, n_qblk, kv_blk0, kv_blk_stride, n_kvblk,
              meta_blk0, name):
    rt = q.shape[0]
    n_heads = 2 * MLA_STEP_PAIRS
    pw = n_heads * HEAD_TILE
    qspec = pl.BlockSpec((tq, pw), lambda s, p, i, t: (q_blk0 + s * q_blk_stride + i, p))
    ktspec = pl.BlockSpec((pw, tk), lambda s, p, i, t: (p, kv_blk0 + s * kv_blk_stride + t))
    vspec = pl.BlockSpec((tk, pw), lambda s, p, i, t: (kv_blk0 + s * kv_blk_stride + t, p))
    mspec = pl.BlockSpec((N_META, pw), lambda s, p, i, t: (meta_blk0 + s, p))
    ospec = pl.BlockSpec((tq, MLA_STEP_PAIRS * LANES), lambda s, p, i, t: (q_blk0 + s * q_blk_stride + i, p))
    return pl.pallas_call(
        _mla_kernel,
        grid=(n_seq, N_PAIRS // MLA_STEP_PAIRS, n_qblk, n_kvblk),
        in_specs=[qspec, ktspec, vspec, mspec, mspec, pl.BlockSpec(memory_space=pl.ANY)],
        out_specs=ospec,
        out_shape=jax.ShapeDtypeStruct((rt, MLA_WIDTH), _F32),
        scratch_shapes=[pltpu.VMEM((n_heads, tq, LANES), _F32), pltpu.VMEM((n_heads, tq, LANES), _F32)],
        input_output_aliases={5: 0},
        compiler_params=_cparams(("parallel", "parallel", "parallel", "arbitrary")),
        name=name,
    )(q, kt, v, k, v, o_prev)


def _mla_attention(q, kt, k, v, layout):
    rt = q.shape[0]
    o = jnp.zeros((rt, MLA_WIDTH), _F32)
    nt = layout["n_tok_total"]
    for gi, (seq0, n_seq, n_tok) in enumerate(layout["groups"]):
        tq = min(MLA_TQ, n_tok)
        tk = min(MLA_TK, n_tok)
        off = layout["tok_off"][seq0]
        assert off % tq == 0 and off % tk == 0 and n_tok % tq == 0 and n_tok % tk == 0
        o = _mla_call(q, kt, k, v, o, tq=tq, tk=tk, n_seq=n_seq, q_blk0=off // tq, q_blk_stride=n_tok // tq,
                      n_qblk=n_tok // tq, kv_blk0=off // tk, kv_blk_stride=n_tok // tk, n_kvblk=n_tok // tk,
                      meta_blk0=nt // N_META + seq0, name=f"mla_tok{gi}")
        o = _mla_call(q, kt, k, v, o, tq=N_META, tk=tk, n_seq=n_seq, q_blk0=nt // N_META + seq0, q_blk_stride=1,
                      n_qblk=1, kv_blk0=off // tk, kv_blk_stride=n_tok // tk, n_kvblk=n_tok // tk,
                      meta_blk0=nt // N_META + seq0, name=f"mla_meta{gi}")
    return o


def _attn_out_kernel(ona_ref, omla_ref, h_ref, g_na_ref, g_mla_ref, w_out_ref, g_ffn_ref, w_r_ref, b_r_ref,
                     h1_ref, m_ref, idx_ref, pos_ref, gate_ref, cnt_ref, cnt_sc, *, n_valid):
    i = pl.program_id(0)
    tb = h_ref.shape[0]

    @pl.when(i == 0)
    def _():
        cnt_sc[...] = jnp.zeros(cnt_sc.shape, _F32)

    n1 = _rms(ona_ref[...], g_na_ref[...]).astype(_BF16)
    n2 = _rms(omla_ref[...], g_mla_ref[...]).astype(_BF16)
    h1 = h_ref[...] + _dot(n1, w_out_ref[0:NA_WIDTH, :]) + _dot(n2, w_out_ref[NA_WIDTH:, :])
    h1_ref[...] = h1
    m = _rms(h1, g_ffn_ref[...])
    m_ref[...] = m.reshape(m_ref.shape)
    m_hi = m.astype(_BF16)
    m_lo = (m - m_hi.astype(_F32)).astype(_BF16)
    hi = _dot(m_hi, w_r_ref[...])
    logits = hi[:, :LANES] + (hi[:, LANES:] + _dot(m_lo, w_r_ref[:, :LANES])) + b_r_ref[...]

    lane = lax.broadcasted_iota(jnp.int32, (tb, LANES), 1).astype(_F32)
    row = lax.broadcasted_iota(jnp.int32, (tb, 1), 0) + i * tb
    valid = jnp.where(row < n_valid, 1.0, 0.0)
    work = logits
    sel = jnp.zeros((tb, LANES), _F32)
    idx_out = jnp.zeros((tb, LANES), _F32)
    top = []
    for kk in range(TOP_K):
        mx = jnp.max(work, axis=-1, keepdims=True)
        idx = jnp.min(jnp.where(work == mx, lane, float(LANES)), axis=-1, keepdims=True)
        hit = lane == idx
        sel = jnp.where(hit, 1.0, sel)
        work = jnp.where(hit, NEG_BIG * 2, work)
        idx_out = jnp.where(lane == kk, idx, idx_out)
        top.append((mx, idx))
    e = [jnp.exp(mx - top[0][0]) for mx, _ in top]
    denom = e[0] + e[1] + e[2] + e[3]
    gate_out = jnp.zeros((tb, LANES), _F32)
    for kk in range(TOP_K):
        gate_out = jnp.where(lane == kk, e[kk] / denom, gate_out)
    gate_ref[...] = gate_out * valid
    idx_ref[...] = idx_out

    sel = sel * valid
    r_i = lax.broadcasted_iota(jnp.int32, (tb, tb), 0)
    c_i = lax.broadcasted_iota(jnp.int32, (tb, tb), 1)
    tri = jnp.where(c_i < r_i, 1.0, 0.0).astype(_BF16)
    pos_full = _dot(tri, sel.astype(_BF16)) + cnt_sc[...]
    pos_out = jnp.zeros((tb, LANES), _F32)
    for kk in range(TOP_K):
        pk = jnp.sum(jnp.where(lane == top[kk][1], pos_full, 0.0), axis=-1, keepdims=True)
        pos_out = jnp.where(lane == kk, pk, pos_out)
    pos_ref[...] = pos_out
    cnt_sc[...] = cnt_sc[...] + jnp.sum(sel, axis=0, keepdims=True)
    cnt_ref[...] = cnt_sc[...]


def _attn_out(o_na, o_mla, h, g_na, g_mla, w_out, g_ffn, w_r, b_r, n_valid):
    rt = h.shape[0]
    tb = ROW_BLOCK
    row = lambda w: pl.BlockSpec((tb, w), lambda i: (i, 0))
    full = lambda a: pl.BlockSpec(a.shape, lambda i: (0,) * a.ndim)
    return pl.pallas_call(
        functools.partial(_attn_out_kernel, n_valid=n_valid),
        grid=(rt // tb,),
        in_specs=[row(NA_WIDTH), row(MLA_WIDTH), row(D_MODEL), full(g_na), full(g_mla), full(w_out), full(g_ffn),
                  full(w_r), full(b_r)],
        out_specs=[row(D_MODEL), pl.BlockSpec((tb,) + ROW_TILE, lambda i: (i, 0, 0)), row(LANES), row(LANES),
                   row(LANES), pl.BlockSpec((1, LANES), lambda i: (0, 0))],
        out_shape=[jax.ShapeDtypeStruct((rt, D_MODEL), _F32), jax.ShapeDtypeStruct((rt,) + ROW_TILE, _F32),
                   jax.ShapeDtypeStruct((rt, LANES), _F32), jax.ShapeDtypeStruct((rt, LANES), _F32),
                   jax.ShapeDtypeStruct((rt, LANES), _F32), jax.ShapeDtypeStruct((1, LANES), _F32)],
        scratch_shapes=[pltpu.VMEM((1, LANES), _F32)],
        compiler_params=_cparams(("arbitrary",)),
        name="attn_out_router",
    )(o_na, o_mla, h, g_na, g_mla, w_out, g_ffn, w_r, b_r)


def _dest_kernel(idx_ref, pos_ref, pstart_ref, dest_ref, src_ref, *, n_valid, trash0):
    i = pl.program_id(0)
    tb = idx_ref.shape[0]
    lane_i = lax.broadcasted_iota(jnp.int32, (tb, LANES), 1)
    lane = lane_i.astype(_F32)
    row = lax.broadcasted_iota(jnp.int32, (tb, 1), 0) + i * tb
    valid = row < n_valid
    idx = idx_ref[...]
    out = pos_ref[...]
    for kk in range(TOP_K):
        start = jnp.sum(jnp.where(lane == idx[:, kk:kk + 1], pstart_ref[...], 0.0), axis=-1, keepdims=True)
        out = jnp.where(lane == kk, out + start, out)
    slot = out.astype(jnp.int32)
    dest_ref[...] = jnp.where(valid, slot, trash0 + (row - n_valid) * TOP_K + lane_i)[:, :TOP_K]
    src_ref[...] = jnp.where(valid, slot, 0)[:, :TOP_K]


def _dest(idx, pos, pstart, n_valid, trash0):
    rt = idx.shape[0]
    tb = ROW_BLOCK
    row = pl.BlockSpec((tb, LANES), lambda i: (i, 0))
    return pl.pallas_call(
        functools.partial(_dest_kernel, n_valid=n_valid, trash0=trash0),
        grid=(rt // tb,),
        in_specs=[row, row, pl.BlockSpec((1, LANES), lambda i: (0, 0))],
        out_specs=[pl.BlockSpec((tb, TOP_K), lambda i: (i, 0))] * 2,
        out_shape=[jax.ShapeDtypeStruct((rt, TOP_K), jnp.int32)] * 2,
        compiler_params=_cparams(("parallel",)),
        name="moe_dest",
    )(idx, pos, pstart)


def _row_copies_wait(ref, n_rows, sem):
    pltpu.make_async_copy(ref.at[pl.ds(0, n_rows)], ref.at[pl.ds(0, n_rows)], sem).wait()


def _dispatch_kernel(free_ref, dest_ref, m_ref, xs_ref, zero_sc, sem_free, sem_rows):
    i = pl.program_id(0)
    tb = m_ref.shape[0]
    n_free = free_ref.shape[0]

    @pl.when(i == 0)
    def _():
        zero_sc[...] = jnp.zeros(zero_sc.shape, zero_sc.dtype)

        def fill(j, carry):
            for u in range(2):
                pltpu.make_async_copy(zero_sc, xs_ref.at[free_ref[2 * j + u]], sem_free).start(priority=u)
            return carry

        lax.fori_loop(0, n_free // 2, fill, 0)
        _row_copies_wait(xs_ref, n_free, sem_free)

    def body(t, carry):
        for kk in range(TOP_K):
            d = dest_ref[0, 0, t * TOP_K + kk]
            pltpu.make_async_copy(m_ref.at[t], xs_ref.at[d], sem_rows).start(priority=kk % 2)
        return carry

    lax.fori_loop(0, tb, body, 0)
    _row_copies_wait(xs_ref, tb * TOP_K, sem_rows)


def _dispatch(free_slots, dest3, m, p_rows):
    rt = m.shape[0]
    tb = ROW_BLOCK
    return pl.pallas_call(
        _dispatch_kernel,
        grid_spec=pltpu.PrefetchScalarGridSpec(
            num_scalar_prefetch=1,
            grid=(rt // tb,),
            in_specs=[pl.BlockSpec((1, 1, tb * TOP_K), lambda i, f: (i, 0, 0), memory_space=pltpu.SMEM),
                      pl.BlockSpec((tb,) + ROW_TILE, lambda i, f: (i, 0, 0))],
            out_specs=pl.BlockSpec(memory_space=pl.ANY),
            scratch_shapes=[pltpu.VMEM(ROW_TILE, _F32), pltpu.SemaphoreType.DMA(()),
                            pltpu.SemaphoreType.DMA(())]),
        out_shape=jax.ShapeDtypeStruct((p_rows,) + ROW_TILE, _F32),
        compiler_params=_cparams(("arbitrary",)),
        name="moe_dispatch",
    )(free_slots, dest3, m)


def _ffn_kernel(be_ref, nu_ref, x_ref, wu_ref, bu_ref, wd_ref, bd_ref, y_ref, wu_sc, wd_sc):
    i = pl.program_id(0)

    @pl.when(i < nu_ref[0])
    def _():
        prev = be_ref[jnp.maximum(i - 1, 0)]

        @pl.when((i == 0) | (be_ref[i] != prev))
        def _():
            wu_sc[...] = wu_ref[0, 0].astype(_BF16)
            wd_sc[...] = wd_ref[0, 0].astype(_BF16)

        x = x_ref[...].reshape(x_ref.shape[0], D_MODEL)
        h = _dot(x.astype(_BF16), wu_sc[...]) + bu_ref[0, 0]
        gate = jnp.minimum(h[:, :D_FF], SWIGLU_LIMIT)
        up = jnp.clip(h[:, D_FF:], -SWIGLU_LIMIT, SWIGLU_LIMIT)
        glu = (0.5 * gate) * (1.0 + jnp.tanh(gate * (0.5 * SWIGLU_ALPHA)))
        y = _dot(((up + 1.0) * glu).astype(_BF16), wd_sc[...]) + bd_ref[0, 0]
        y_ref[...] = y.reshape(y_ref.shape)

    @pl.when(i >= nu_ref[0])
    def _():
        y_ref[...] = jnp.zeros(y_ref.shape, y_ref.dtype)


def _ffn(block_e, n_used, xs, w_up, b_up, w_down, b_down, n_blocks, layer):
    bm = EXPERT_BLOCK
    return pl.pallas_call(
        _ffn_kernel,
        grid_spec=pltpu.PrefetchScalarGridSpec(
            num_scalar_prefetch=2,
            grid=(n_blocks,),
            in_specs=[pl.BlockSpec((bm,) + ROW_TILE, lambda i, be, nu: (i, 0, 0)),
                      pl.BlockSpec((1, 1, D_MODEL, 2 * D_FF), lambda i, be, nu: (layer, be[i], 0, 0)),
                      pl.BlockSpec((1, 1, 1, 2 * D_FF), lambda i, be, nu: (layer, be[i], 0, 0)),
                      pl.BlockSpec((1, 1, D_FF, D_MODEL), lambda i, be, nu: (layer, be[i], 0, 0)),
                      pl.BlockSpec((1, 1, 1, D_MODEL), lambda i, be, nu: (layer, be[i], 0, 0))],
            out_specs=pl.BlockSpec((bm,) + ROW_TILE, lambda i, be, nu: (i, 0, 0)),
            scratch_shapes=[pltpu.VMEM((D_MODEL, 2 * D_FF), _BF16), pltpu.VMEM((D_FF, D_MODEL), _BF16)]),
        out_shape=jax.ShapeDtypeStruct((n_blocks * bm,) + ROW_TILE, _F32),
        compiler_params=_cparams(("arbitrary",)),
        name="moe_ffn",
    )(block_e, n_used, xs, w_up, b_up, w_down, b_down)


def _combine_kernel(dest_ref, gate_ref, h1_ref, g_ref, y_ref, o_ref, buf_sc, sem, *, final):
    tb = h1_ref.shape[0]

    def body(t, carry):
        for kk in range(TOP_K):
            d = dest_ref[0, 0, t * TOP_K + kk]
            pltpu.make_async_copy(y_ref.at[d], buf_sc.at[kk, t], sem).start(priority=kk % 2)
        return carry

    lax.fori_loop(0, tb, body, 0)
    _row_copies_wait(y_ref, tb * TOP_K, sem)
    gate = gate_ref[...]
    out = h1_ref[...]
    for kk in range(TOP_K):
        out = out + gate[:, kk:kk + 1] * buf_sc[kk].reshape(tb, D_MODEL)
    if final:
        out = _rms(out, g_ref[...])
    o_ref[...] = out


def _combine(dest4, gate, h1, g_final, y, final, tb, blk0, nblk):
    dest3 = dest4[blk0 * tb:(blk0 + nblk) * tb].reshape(nblk, 1, tb * TOP_K)
    return pl.pallas_call(
        functools.partial(_combine_kernel, final=final),
        grid=(nblk,),
        in_specs=[pl.BlockSpec((1, 1, tb * TOP_K), lambda i: (i, 0, 0), memory_space=pltpu.SMEM),
                  pl.BlockSpec((tb, LANES), lambda i: (blk0 + i, 0)),
                  pl.BlockSpec((tb, D_MODEL), lambda i: (blk0 + i, 0)),
                  pl.BlockSpec((1, D_MODEL), lambda i: (0, 0)),
                  pl.BlockSpec(memory_space=pl.ANY)],
        out_specs=pl.BlockSpec((tb, D_MODEL), lambda i: (i, 0)),
        out_shape=jax.ShapeDtypeStruct((nblk * tb, D_MODEL), _F32),
        scratch_shapes=[pltpu.VMEM((TOP_K, tb) + ROW_TILE, _F32), pltpu.SemaphoreType.DMA(())],
        compiler_params=_cparams(("arbitrary",)),
        name="moe_combine",
    )(dest3, gate, h1, g_final, y)


def _make_layout(seq_tokens, groups):
    n_seq = len(seq_tokens)
    tok_off = np.concatenate([[0], np.cumsum(seq_tokens)]).astype(np.int64)
    nt = int(tok_off[-1])
    n_valid = nt + n_seq * N_META
    rt = -(-n_valid // ROW_BLOCK) * ROW_BLOCK
    pos = np.zeros((rt,), np.float32)
    cidx, var, midx = [], [], []
    for s, n in enumerate(seq_tokens):
        assert n % NA_QBLOCK == 0 and n // NA_QBLOCK >= NA_KBLOCKS
        pos[tok_off[s]:tok_off[s] + n] = N_META + np.arange(n)
        pos[nt + s * N_META:nt + (s + 1) * N_META] = np.arange(N_META)
        nb = n // NA_QBLOCK
        b0 = int(tok_off[s]) // NA_QBLOCK
        for b in range(nb):
            cidx.append(b0 + min(max(b, 1), nb - 2))
            var.append(0 if b == 0 else (2 if b == nb - 1 else 1))
            midx.append(nt // N_META + s)
    return {
        "seq_tokens": tuple(seq_tokens), "groups": tuple(groups), "tok_off": tuple(int(v) for v in tok_off),
        "n_tok_total": nt, "n_valid": n_valid, "rt": rt, "pos": pos,
        "na_cidx": jnp.asarray(cidx, jnp.int32), "na_var": jnp.asarray(var, jnp.int32),
        "na_midx": jnp.asarray(midx, jnp.int32),
        "seq_midx": jnp.asarray([nt // N_META + s for s in range(n_seq)], jnp.int32),
    }


def _rope_tables(pos):
    freqs = jnp.power(ROPE_THETA, -jnp.arange(0, QK_ROPE, 2, dtype=_F32) / QK_ROPE)
    ang = jnp.asarray(pos)[:, None] * freqs[None, :]
    cos, sin = jnp.cos(ang), jnp.sin(ang)
    rt = pos.shape[0]
    pad = LANES - QK_NOPE - QK_ROPE
    cos_t = jnp.concatenate([jnp.ones((rt, QK_NOPE), _F32), cos, cos, jnp.zeros((rt, pad), _F32)], axis=1)
    sin_t = jnp.concatenate([jnp.zeros((rt, QK_NOPE), _F32), sin, sin, jnp.zeros((rt, pad), _F32)], axis=1)
    return cos_t, sin_t


def _layer_weights(w_in, w_uq, w_ukv):
    half = QK_ROPE // 2
    s2 = 3 * NA_WIDTH
    kr_cols = w_in[:, s2 + Q_LORA + KV_LORA:]
    w_in_p = jnp.concatenate([w_in[:, :NA_WIDTH] * (NA_HEAD_DIM ** -0.5), w_in[:, NA_WIDTH:s2 + Q_LORA + KV_LORA],
                              kr_cols, jnp.zeros((D_MODEL, LANES - QK_ROPE), _F32)], axis=1).astype(_BF16)
    dq = QK_NOPE + QK_ROPE
    wq = w_uq.reshape(Q_LORA, MLA_HEADS, dq)
    zq = jnp.zeros((Q_LORA, MLA_HEADS, LANES - dq), _F32)
    q_plain = jnp.concatenate([wq, zq], axis=2)
    q_rot = jnp.concatenate([jnp.zeros((Q_LORA, MLA_HEADS, QK_NOPE), _F32), -wq[:, :, QK_NOPE + half:],
                             wq[:, :, QK_NOPE:QK_NOPE + half], zq], axis=2)
    w_q_p = jnp.concatenate([q_plain.reshape(Q_LORA, -1), q_rot.reshape(Q_LORA, -1)], axis=1).astype(_BF16)
    wkv = w_ukv.reshape(KV_LORA, MLA_HEADS, QK_NOPE + V_HEAD)
    k_plain = jnp.concatenate([wkv[:, :, :QK_NOPE], jnp.zeros((KV_LORA, MLA_HEADS, LANES - QK_NOPE), _F32)], axis=2)
    wv = wkv[:, :, QK_NOPE:].reshape(KV_LORA, N_PAIRS, 2, V_HEAD)
    zv = jnp.zeros((KV_LORA, N_PAIRS, LANES - V_HEAD), _F32)
    v_even = jnp.concatenate([wv[:, :, 0], zv], axis=2)
    v_odd = jnp.concatenate([zv, wv[:, :, 1]], axis=2)
    v_plain = jnp.stack([v_even, v_odd], axis=2)
    w_kv_p = jnp.concatenate([k_plain.reshape(KV_LORA, -1), v_plain.reshape(KV_LORA, -1)], axis=1).astype(_BF16)
    return w_in_p, w_q_p, w_kv_p


def _const_tables():
    half = QK_ROPE // 2
    width = MLA_HEADS * HEAD_TILE
    r_plain = np.zeros((LANES, width), np.float32)
    r_rot = np.zeros((LANES, width), np.float32)
    vone = np.zeros((1, width), np.float32)
    for h in range(MLA_HEADS):
        base = h * HEAD_TILE + QK_NOPE
        for j in range(QK_ROPE):
            r_plain[j, base + j] = 1.0
        for j in range(half):
            r_rot[half + j, base + j] = -1.0
            r_rot[j, base + half + j] = 1.0
        vone[0, h * HEAD_TILE + (V_HEAD if h % 2 == 0 else 0)] = 1.0
    w_kr_p = jnp.asarray(np.concatenate([r_plain, r_rot], axis=1), _BF16)
    return w_kr_p, jnp.asarray(vone)


def _moe_plan(counts, n_blocks, n_free):
    counts = counts[0, :N_EXPERTS].astype(jnp.int32)
    padded = (counts + EXPERT_BLOCK - 1) // EXPERT_BLOCK * EXPERT_BLOCK
    pend = jnp.cumsum(padded)
    pstart = pend - padded
    first_row = jnp.arange(n_blocks, dtype=jnp.int32) * EXPERT_BLOCK
    block_e = jnp.minimum(jnp.sum((pend[None, :] <= first_row[:, None]).astype(jnp.int32), axis=1), N_EXPERTS - 1)
    n_used = (pend[-1:] // EXPERT_BLOCK).astype(jnp.int32)
    pstart_row = jnp.zeros((1, LANES), _F32).at[0, :N_EXPERTS].set(pstart.astype(_F32))
    gap = jnp.concatenate([padded - counts, (n_blocks * EXPERT_BLOCK - pend[-1:])])
    gap_end = jnp.cumsum(gap)
    gap_first = jnp.concatenate([pstart + counts, pend[-1:]])
    j = jnp.arange(n_free, dtype=jnp.int32)
    seg = jnp.sum((gap_end[None, :] <= j[:, None]).astype(jnp.int32), axis=1)
    free_slots = (gap_first[seg] + j - (gap_end - gap)[seg]).astype(jnp.int32)
    return pstart_row, block_e, n_used, free_slots


def _forward(h, layout, meta_tokens, g_attn, w_in, g_q, w_uq, g_kv, w_ukv, rpb, g_out_na, g_out_mla, w_out,
             g_ffn, w_router, b_router, w_up, b_up, w_down, b_down, g_final):
    del meta_tokens
    depth = w_in.shape[0]
    rt = layout["rt"]
    n_valid = layout["n_valid"]
    cos_t, sin_t = _rope_tables(layout["pos"])
    w_kr_p, vone = _const_tables()
    n_assign = n_valid * TOP_K
    n_blocks = -(-n_assign // EXPERT_BLOCK) + N_EXPERTS
    trash0 = n_blocks * EXPERT_BLOCK
    p_rows = trash0 + (rt - n_valid) * TOP_K
    n_free = trash0 - n_assign
    row2 = lambda a: a.reshape(1, -1)
    for l in range(depth):
        w_in_p, w_q_p, w_kv_p = _layer_weights(w_in[l], w_uq[l], w_ukv[l])
        qna, kna, vna, qm, km, vm, kmt = _attn_in(h, cos_t, sin_t, row2(g_attn[l]), w_in_p, row2(g_q[l]), w_q_p,
                                             row2(g_kv[l]), w_kv_p, w_kr_p, vone)
        o_na = _na_attention(qna, kna, vna, _na_bias(rpb[l]), layout)
        o_mla = _mla_attention(qm, kmt, km, vm, layout)
        w_r32 = jnp.concatenate([w_router[l], jnp.zeros((D_MODEL, LANES - N_EXPERTS), _F32)], axis=1)
        w_r_hi = w_r32.astype(_BF16)
        w_r = jnp.concatenate([w_r_hi, (w_r32 - w_r_hi.astype(_F32)).astype(_BF16)], axis=1)
        b_r = jnp.concatenate([b_router[l], jnp.full((LANES - N_EXPERTS,), NEG_BIG, _F32)]).reshape(1, LANES)
        h1, m, idx, pos, gate, counts = _attn_out(o_na, o_mla, h, row2(g_out_na[l]), row2(g_out_mla[l]),
                                                  w_out[l].astype(_BF16), row2(g_ffn[l]), w_r, b_r, n_valid)
        pstart_row, block_e, n_used, free_slots = _moe_plan(counts, n_blocks, n_free)
        dest4, dest4c = _dest(idx, pos, pstart_row, n_valid, trash0)
        xs = _dispatch(free_slots, dest4.reshape(rt // ROW_BLOCK, 1, ROW_BLOCK * TOP_K), m, p_rows)
        y = _ffn(block_e, n_used, xs, w_up, b_up.reshape(depth, N_EXPERTS, 1, -1), w_down,
                 b_down.reshape(depth, N_EXPERTS, 1, -1), n_blocks, l)
        if l < depth - 1:
            h = _combine(dest4c, gate, h1, row2(g_final), y, False, ROW_BLOCK, 0, rt // ROW_BLOCK)
    outs = []
    for seq0, n_seq, n_tok in layout["groups"]:
        tb = min(FINAL_BLOCK, n_tok)
        outs.append(_combine(dest4c, gate, h1, row2(g_final), y, True, tb, layout["tok_off"][seq0] // tb,
                             n_seq * n_tok // tb))
    return outs


def kernel(x_prompt, x_sample, meta_tokens, g_attn, w_in, g_q, w_uq, g_kv, w_ukv, rpb, g_out_na, g_out_mla, w_out,
           g_ffn, w_router, b_router, w_up, b_up, w_down, b_down, g_final):
    bp, lp, _ = x_prompt.shape
    bs, ls, _ = x_sample.shape
    seq_tokens = [lp] * bp + [ls] * bs
    layout = _make_layout(seq_tokens, [(0, bp, lp), (bp, bs, ls)])
    n_seq = len(seq_tokens)
    rt, n_valid = layout["rt"], layout["n_valid"]
    meta = jnp.broadcast_to(meta_tokens[None], (n_seq, N_META, D_MODEL)).reshape(n_seq * N_META, D_MODEL)
    h = jnp.concatenate([x_prompt.reshape(bp * lp, D_MODEL), x_sample.reshape(bs * ls, D_MODEL), meta,
                         jnp.zeros((rt - n_valid, D_MODEL), _F32)], axis=0)
    out = _forward(h, layout, meta_tokens, g_attn, w_in, g_q, w_uq, g_kv, w_ukv, rpb, g_out_na, g_out_mla, w_out,
                   g_ffn, w_router, b_router, w_up, b_up, w_down, b_down, g_final)
    return (out[0].reshape(bp, lp, D_MODEL), out[1].reshape(bs, ls, D_MODEL))
```

```python
import functools

import numpy as np
import jax
import jax.numpy as jnp
from jax import lax
from jax.experimental import pallas as pl
from jax.experimental.pallas import tpu as pltpu

D_MODEL = 1024
GRID_W = 64
N_META = 16
NA_HEADS = 8
NA_HEAD_DIM = 64
NA_WIN_H = 8
NA_WIN_W = 16
NA_WIDTH = NA_HEADS * NA_HEAD_DIM
MLA_HEADS = 8
QK_NOPE = 64
QK_ROPE = 32
V_HEAD = 64
Q_LORA = 256
KV_LORA = 128
ROPE_THETA = 10000.0
MLA_WIDTH = MLA_HEADS * V_HEAD
N_EXPERTS = 32
TOP_K = 4
D_FF = 1024
SWIGLU_LIMIT = 7.0
SWIGLU_ALPHA = 1.702
EPS = 1e-6

LANES = 128
HEAD_TILE = LANES
N_PAIRS = NA_HEADS // 2
ROW_BLOCK = 768
FINAL_BLOCK = 1024
NA_QROWS = 4
NA_QBLOCK = NA_QROWS * GRID_W
NA_KBLOCKS = 3
NA_STEP_PAIRS = 4
EXPERT_BLOCK = 512
ROW_TILE = (D_MODEL // LANES, LANES)
MLA_TQ = 1024
MLA_TK = 2048
MLA_CHUNK = 2048
MLA_STEP_PAIRS = 2
NEG_BIG = -1e30
LOG2_E = 1.4426950408889634
VMEM_LIMIT = 56 * 1024 * 1024

_F32 = jnp.float32
_BF16 = jnp.bfloat16


def _cparams(sem):
    return pltpu.CompilerParams(dimension_semantics=sem, vmem_limit_bytes=VMEM_LIMIT)


def _rms(x, g):
    return x * lax.rsqrt(jnp.mean(x * x, axis=-1, keepdims=True) + EPS) * g


def _dot(a, b):
    return jnp.dot(a, b, preferred_element_type=_F32)


def _dot_nt(a, b):
    return lax.dot_general(a, b, (((1,), (1,)), ((), ())), preferred_element_type=_F32)


def _attn_in_kernel(h_ref, cos_ref, sin_ref, g_attn_ref, w_in_ref, g_q_ref, w_q_ref, g_kv_ref,
                    w_kv_ref, w_kr_ref, vone_ref,
                    qna_ref, kna_ref, vna_ref, qm_ref, km_ref, vm_ref, kmt_ref):
    a = _rms(h_ref[...], g_attn_ref[...]).astype(_BF16)
    proj = _dot(a, w_in_ref[...])
    qna_ref[...] = (proj[:, 0:NA_WIDTH] * LOG2_E).astype(_BF16)
    kna_ref[...] = proj[:, NA_WIDTH:2 * NA_WIDTH].astype(_BF16)
    vna_ref[...] = proj[:, 2 * NA_WIDTH:3 * NA_WIDTH].astype(_BF16)
    s2 = 3 * NA_WIDTH
    cq = proj[:, s2:s2 + Q_LORA]
    ckv = proj[:, s2 + Q_LORA:s2 + Q_LORA + KV_LORA]
    kr = proj[:, s2 + Q_LORA + KV_LORA:]
    cos = jnp.concatenate([cos_ref[...]] * MLA_HEADS, axis=1)
    sin = jnp.concatenate([sin_ref[...]] * MLA_HEADS, axis=1)
    width = MLA_HEADS * HEAD_TILE
    q2 = _dot(_rms(cq, g_q_ref[...]).astype(_BF16), w_q_ref[...])
    scale = (QK_NOPE + QK_ROPE) ** -0.5 * LOG2_E
    qm_ref[...] = ((q2[:, :width] * cos + q2[:, width:] * sin) * scale).astype(_BF16)
    kv2 = _dot(_rms(ckv, g_kv_ref[...]).astype(_BF16), w_kv_ref[...])
    kr2 = _dot(kr.astype(_BF16), w_kr_ref[...])
    km = (kv2[:, :width] + kr2[:, :width]) * cos + kr2[:, width:] * sin
    km_ref[...] = km.astype(_BF16)
    kmt_ref[...] = km.T.astype(_BF16)
    vm_ref[...] = (kv2[:, width:] + vone_ref[...]).astype(_BF16)


def _attn_in(h, cos_t, sin_t, g_attn, w_in_p, g_q, w_q_p, g_kv, w_kv_p, w_kr_p, vone):
    rt = h.shape[0]
    tb = ROW_BLOCK
    width = MLA_HEADS * HEAD_TILE
    row = lambda w: pl.BlockSpec((tb, w), lambda i: (i, 0))
    full = lambda a: pl.BlockSpec(a.shape, lambda i: (0,) * a.ndim)
    outs = ([jax.ShapeDtypeStruct((rt, NA_WIDTH), _BF16)] * 3 + [jax.ShapeDtypeStruct((rt, width), _BF16)] * 3
            + [jax.ShapeDtypeStruct((width, rt), _BF16)])
    return pl.pallas_call(
        _attn_in_kernel,
        grid=(rt // tb,),
        in_specs=[row(D_MODEL), row(LANES), row(LANES), full(g_attn), full(w_in_p), full(g_q), full(w_q_p),
                  full(g_kv), full(w_kv_p), full(w_kr_p), full(vone)],
        out_specs=[row(NA_WIDTH)] * 3 + [row(width)] * 3 + [pl.BlockSpec((width, tb), lambda i: (0, i))],
        out_shape=outs,
        compiler_params=_cparams(("parallel",)),
        name="attn_in",
    )(h, cos_t, sin_t, g_attn, w_in_p, g_q, w_q_p, g_kv, w_kv_p, w_kr_p, vone)


def _na_kernel(cidx_ref, var_ref, midx_ref, q_ref, kp_ref, kc_ref, kn_ref, vp_ref, vc_ref, vn_ref,
               km_ref, vm_ref, bias_ref, oin_ref, o_ref):
    del cidx_ref, var_ref, midx_ref, oin_ref
    lane = lax.broadcasted_iota(jnp.int32, (1, LANES), 1)
    for pair in range(q_ref.shape[1] // LANES):
        tile = slice(pair * LANES, (pair + 1) * LANES)
        q = q_ref[:, tile]
        ks = (kp_ref[:, tile], kc_ref[:, tile], kn_ref[:, tile])
        vs = (vp_ref[:, tile], vc_ref[:, tile], vn_ref[:, tile])
        km = km_ref[:, tile]
        vm = vm_ref[:, tile]
        outs = []
        for hh in range(2):
            in_head = (lane >= hh * NA_HEAD_DIM) & (lane < (hh + 1) * NA_HEAD_DIM)
            qh = jnp.where(in_head, q, jnp.zeros_like(q))
            s_loc = jnp.concatenate([_dot_nt(qh, k) for k in ks], axis=1) + bias_ref[0, 2 * pair + hh]
            s_met = _dot_nt(qh, km)
            m = jnp.maximum(jnp.max(s_loc, axis=-1, keepdims=True), jnp.max(s_met, axis=-1, keepdims=True))
            p_loc = jnp.exp2(s_loc - m)
            p_met = jnp.exp2(s_met - m)
            l = jnp.sum(p_loc, axis=-1, keepdims=True) + jnp.sum(p_met, axis=-1, keepdims=True)
            o = _dot(p_met.astype(_BF16), vm)
            for j in range(NA_KBLOCKS):
                o = o + _dot(p_loc[:, j * NA_QBLOCK:(j + 1) * NA_QBLOCK].astype(_BF16), vs[j])
            outs.append(o / l)
        o_ref[:, tile] = jnp.where(lane < NA_HEAD_DIM, outs[0], outs[1])


def _na_meta_kernel(midx_ref, q_ref, k_ref, v_ref, oin_ref, o_ref):
    del midx_ref, oin_ref
    q = q_ref[...]
    k = k_ref[...]
    v = v_ref[...]
    lane = lax.broadcasted_iota(jnp.int32, (1, LANES), 1)
    outs = []
    for hh in range(2):
        in_head = (lane >= hh * NA_HEAD_DIM) & (lane < (hh + 1) * NA_HEAD_DIM)
        s = _dot_nt(jnp.where(in_head, q, jnp.zeros_like(q)), k)
        p = jnp.exp2(s - jnp.max(s, axis=-1, keepdims=True))
        outs.append(_dot(p.astype(_BF16), v) / jnp.sum(p, axis=-1, keepdims=True))
    o_ref[...] = jnp.where(lane < NA_HEAD_DIM, outs[0], outs[1])


def _na_bias(rpb_l):
    n_kr = NA_KBLOCKS * NA_QROWS
    qr = np.arange(NA_QROWS)[:, None]
    kr = np.arange(n_kr)[None, :]
    qc = np.arange(GRID_W)[:, None]
    kc = np.arange(GRID_W)[None, :]
    cs = np.clip(qc - NA_WIN_W // 2, 0, GRID_W - NA_WIN_W)
    col_ok = (kc >= cs) & (kc < cs + NA_WIN_W)
    dc = np.clip(kc - qc + NA_WIN_W - 1, 0, 2 * NA_WIN_W - 2)
    sel_c = (np.arange(2 * NA_WIN_W - 1)[None, None, :] == dc[:, :, None]) & col_ok[:, :, None]
    nrows = 4 * n_kr
    sel_r, ok = [], []
    for r0, k0 in ((0, 0), (NA_QROWS, 0), (nrows - NA_QROWS, nrows - n_kr)):
        r = r0 + qr
        key_row = k0 + kr
        rs = np.clip(r - NA_WIN_H // 2, 0, nrows - NA_WIN_H)
        row_ok = (key_row >= rs) & (key_row < rs + NA_WIN_H)
        dr = np.clip(key_row - r + NA_WIN_H - 1, 0, 2 * NA_WIN_H - 2)
        sel_r.append((np.arange(2 * NA_WIN_H - 1)[None, None, :] == dr[:, :, None]) & row_ok[:, :, None])
        ok.append(row_ok[:, None, :, None] & col_ok[None, :, None, :])
    sel_r = jnp.asarray(np.stack(sel_r), _F32)
    sel_c = jnp.asarray(sel_c, _F32)
    b = jnp.einsum("vqkd,hde,cxe->vhqckx", sel_r, rpb_l.astype(_F32), sel_c, precision=lax.Precision.HIGHEST)
    b = jnp.where(jnp.asarray(np.stack(ok))[:, None], b * LOG2_E, NEG_BIG)
    return b.reshape(3, NA_HEADS, NA_QBLOCK, NA_KBLOCKS * NA_QBLOCK)


def _na_attention(q, k, v, bias, layout):
    rt = q.shape[0]
    cidx, var, midx = layout["na_cidx"], layout["na_var"], layout["na_midx"]
    nblk = cidx.shape[0]
    pw = NA_STEP_PAIRS * LANES
    qspec = pl.BlockSpec((NA_QBLOCK, pw), lambda p, b, c, vr, m: (b, p))
    kspec = lambda d: pl.BlockSpec((NA_QBLOCK, pw), lambda p, b, c, vr, m: (c[b] + d, p))
    mspec = pl.BlockSpec((N_META, pw), lambda p, b, c, vr, m: (m[b], p))
    bspec = pl.BlockSpec((1, 2 * NA_STEP_PAIRS, NA_QBLOCK, NA_KBLOCKS * NA_QBLOCK),
                         lambda p, b, c, vr, m: (vr[b], p, 0, 0))
    o = pl.pallas_call(
        _na_kernel,
        grid_spec=pltpu.PrefetchScalarGridSpec(
            num_scalar_prefetch=3,
            grid=(N_PAIRS // NA_STEP_PAIRS, nblk),
            in_specs=[qspec, kspec(-1), kspec(0), kspec(1), kspec(-1), kspec(0), kspec(1), mspec, mspec, bspec,
                      pl.BlockSpec(memory_space=pl.ANY)],
            out_specs=qspec),
        out_shape=jax.ShapeDtypeStruct((rt, NA_WIDTH), _F32),
        input_output_aliases={13: 0},
        compiler_params=_cparams(("parallel", "parallel")),
        name="na_attn",
    )(cidx, var, midx, q, k, k, k, v, v, v, k, v, bias, jnp.zeros((rt, NA_WIDTH), _F32))
    smidx = layout["seq_midx"]
    mq = pl.BlockSpec((N_META, LANES), lambda p, s, m: (m[s], p))
    return pl.pallas_call(
        _na_meta_kernel,
        grid_spec=pltpu.PrefetchScalarGridSpec(
            num_scalar_prefetch=1,
            grid=(N_PAIRS, smidx.shape[0]),
            in_specs=[mq, mq, mq, pl.BlockSpec(memory_space=pl.ANY)],
            out_specs=mq),
        out_shape=jax.ShapeDtypeStruct((rt, NA_WIDTH), _F32),
        input_output_aliases={4: 0},
        compiler_params=_cparams(("parallel", "parallel")),
        name="na_meta",
    )(smidx, q, k, v, o)


def _online_update(m_prev, acc, s, v):
    parts = [s[:, j * LANES:(j + 1) * LANES] for j in range(s.shape[1] // LANES)]
    mx = parts[0]
    for part in parts[1:]:
        mx = jnp.maximum(mx, part)
    m_new = jnp.maximum(m_prev, jnp.max(mx, axis=-1, keepdims=True))
    p = jnp.concatenate([jnp.exp2(part - m_new) for part in parts], axis=1)
    acc = jnp.exp2(m_prev - m_new) * acc + _dot(p.astype(_BF16), v)
    return m_new, acc


def _mla_kernel(q_ref, kt_ref, v_ref, km_ref, vm_ref, oin_ref, o_ref, m_sc, acc_sc):
    del oin_ref
    t = pl.program_id(3)
    nt = pl.num_programs(3)
    tk = v_ref.shape[0]
    chunk = min(MLA_CHUNK, tk)

    n_heads = m_sc.shape[0]

    @pl.when(t == 0)
    def _():
        for hh in range(n_heads):
            tile = slice(hh * HEAD_TILE, (hh + 1) * HEAD_TILE)
            s = _dot_nt(q_ref[:, tile], km_ref[:, tile])
            m = jnp.max(s, axis=-1, keepdims=True)
            m_sc[hh] = jnp.broadcast_to(m, m_sc.shape[1:])
            acc_sc[hh] = _dot(jnp.exp2(s - m).astype(_BF16), vm_ref[:, tile])

    for hh in range(n_heads):
        tile = slice(hh * HEAD_TILE, (hh + 1) * HEAD_TILE)
        q = q_ref[:, tile]
        m, acc = m_sc[hh], acc_sc[hh]
        for c in range(tk // chunk):
            cols = slice(c * chunk, (c + 1) * chunk)
            m, acc = _online_update(m, acc, _dot(q, kt_ref[tile, cols]), v_ref[cols, tile])
        m_sc[hh] = m
        acc_sc[hh] = acc

    @pl.when(t == nt - 1)
    def _():
        lane = lax.broadcasted_iota(jnp.int32, (1, LANES), 1)
        for pair in range(n_heads // 2):
            outs = [acc_sc[2 * pair], acc_sc[2 * pair + 1]]
            l0 = outs[0][:, V_HEAD:V_HEAD + 1]
            l1 = outs[1][:, 0:1]
            o_ref[:, pair * LANES:(pair + 1) * LANES] = jnp.where(lane < V_HEAD, outs[0] / l0, outs[1] / l1)


def _mla_call(q, kt, k, v, o_prev, *, tq, tk, n_seq, q_blk0, q_blk_stride, n_qblk, kv_blk0, kv_blk_stride, n_kvblk,
              meta_blk0, name):
    rt = q.shape[0]
    n_heads = 2 * MLA_STEP_PAIRS
    pw = n_heads * HEAD_TILE
    qspec = pl.BlockSpec((tq, pw), lambda s, p, i, t: (q_blk0 + s * q_blk_stride + i, p))
    ktspec = pl.BlockSpec((pw, tk), lambda s, p, i, t: (p, kv_blk0 + s * kv_blk_stride + t))
    vspec = pl.BlockSpec((tk, pw), lambda s, p, i, t: (kv_blk0 + s * kv_blk_stride + t, p))
    mspec = pl.BlockSpec((N_META, pw), lambda s, p, i, t: (meta_blk0 + s, p))
    ospec = pl.BlockSpec((tq, MLA_STEP_PAIRS * LANES), lambda s, p, i, t: (q_blk0 + s * q_blk_stride + i, p))
    return pl.pallas_call(
        _mla_kernel,
        grid=(n_seq, N_PAIRS // MLA_STEP_PAIRS, n_qblk, n_kvblk),
        in_specs=[qspec, ktspec, vspec, mspec, mspec, pl.BlockSpec(memory_space=pl.ANY)],
        out_specs=ospec,
        out_shape=jax.ShapeDtypeStruct((rt, MLA_WIDTH), _F32),
        scratch_shapes=[pltpu.VMEM((n_heads, tq, LANES), _F32), pltpu.VMEM((n_heads, tq, LANES), _F32)],
        input_output_aliases={5: 0},
        compiler_params=_cparams(("parallel", "parallel", "parallel", "arbitrary")),
        name=name,
    )(q, kt, v, k, v, o_prev)


def _mla_attention(q, kt, k, v, layout):
    rt = q.shape[0]
    o = jnp.zeros((rt, MLA_WIDTH), _F32)
    nt = layout["n_tok_total"]
    for gi, (seq0, n_seq, n_tok) in enumerate(layout["groups"]):
        tq = min(MLA_TQ, n_tok)
        tk = min(MLA_TK, n_tok)
        off = layout["tok_off"][seq0]
        assert off % tq == 0 and off % tk == 0 and n_tok % tq == 0 and n_tok % tk == 0
        o = _mla_call(q, kt, k, v, o, tq=tq, tk=tk, n_seq=n_seq, q_blk0=off // tq, q_blk_stride=n_tok // tq,
                      n_qblk=n_tok // tq, kv_blk0=off // tk, kv_blk_stride=n_tok // tk, n_kvblk=n_tok // tk,
                      meta_blk0=nt // N_META + seq0, name=f"mla_tok{gi}")
        o = _mla_call(q, kt, k, v, o, tq=N_META, tk=tk, n_seq=n_seq, q_blk0=nt // N_META + seq0, q_blk_stride=1,
                      n_qblk=1, kv_blk0=off // tk, kv_blk_stride=n_tok // tk, n_kvblk=n_tok // tk,
                      meta_blk0=nt // N_META + seq0, name=f"mla_meta{gi}")
    return o


def _attn_out_kernel(ona_ref, omla_ref, h_ref, g_na_ref, g_mla_ref, w_out_ref, g_ffn_ref, w_r_ref, b_r_ref,
                     h1_ref, m_ref, idx_ref, pos_ref, gate_ref, cnt_ref, cnt_sc, *, n_valid):
    i = pl.program_id(0)
    tb = h_ref.shape[0]

    @pl.when(i == 0)
    def _():
        cnt_sc[...] = jnp.zeros(cnt_sc.shape, _F32)

    n1 = _rms(ona_ref[...], g_na_ref[...]).astype(_BF16)
    n2 = _rms(omla_ref[...], g_mla_ref[...]).astype(_BF16)
    h1 = h_ref[...] + _dot(n1, w_out_ref[0:NA_WIDTH, :]) + _dot(n2, w_out_ref[NA_WIDTH:, :])
    h1_ref[...] = h1
    m = _rms(h1, g_ffn_ref[...])
    m_ref[...] = m.reshape(m_ref.shape)
    m_hi = m.astype(_BF16)
    m_lo = (m - m_hi.astype(_F32)).astype(_BF16)
    hi = _dot(m_hi, w_r_ref[...])
    logits = hi[:, :LANES] + (hi[:, LANES:] + _dot(m_lo, w_r_ref[:, :LANES])) + b_r_ref[...]

    lane = lax.broadcasted_iota(jnp.int32, (tb, LANES), 1).astype(_F32)
    row = lax.broadcasted_iota(jnp.int32, (tb, 1), 0) + i * tb
    valid = jnp.where(row < n_valid, 1.0, 0.0)
    work = logits
    sel = jnp.zeros((tb, LANES), _F32)
    idx_out = jnp.zeros((tb, LANES), _F32)
    top = []
    for kk in range(TOP_K):
        mx = jnp.max(work, axis=-1, keepdims=True)
        idx = jnp.min(jnp.where(work == mx, lane, float(LANES)), axis=-1, keepdims=True)
        hit = lane == idx
        sel = jnp.where(hit, 1.0, sel)
        work = jnp.where(hit, NEG_BIG * 2, work)
        idx_out = jnp.where(lane == kk, idx, idx_out)
        top.append((mx, idx))
    e = [jnp.exp(mx - top[0][0]) for mx, _ in top]
    denom = e[0] + e[1] + e[2] + e[3]
    gate_out = jnp.zeros((tb, LANES), _F32)
    for kk in range(TOP_K):
        gate_out = jnp.where(lane == kk, e[kk] / denom, gate_out)
    gate_ref[...] = gate_out * valid
    idx_ref[...] = idx_out

    sel = sel * valid
    r_i = lax.broadcasted_iota(jnp.int32, (tb, tb), 0)
    c_i = lax.broadcasted_iota(jnp.int32, (tb, tb), 1)
    tri = jnp.where(c_i < r_i, 1.0, 0.0).astype(_BF16)
    pos_full = _dot(tri, sel.astype(_BF16)) + cnt_sc[...]
    pos_out = jnp.zeros((tb, LANES), _F32)
    for kk in range(TOP_K):
        pk = jnp.sum(jnp.where(lane == top[kk][1], pos_full, 0.0), axis=-1, keepdims=True)
        pos_out = jnp.where(lane == kk, pk, pos_out)
    pos_ref[...] = pos_out
    cnt_sc[...] = cnt_sc[...] + jnp.sum(sel, axis=0, keepdims=True)
    cnt_ref[...] = cnt_sc[...]


def _attn_out(o_na, o_mla, h, g_na, g_mla, w_out, g_ffn, w_r, b_r, n_valid):
    rt = h.shape[0]
    tb = ROW_BLOCK
    row = lambda w: pl.BlockSpec((tb, w), lambda i: (i, 0))
    full = lambda a: pl.BlockSpec(a.shape, lambda i: (0,) * a.ndim)
    return pl.pallas_call(
        functools.partial(_attn_out_kernel, n_valid=n_valid),
        grid=(rt // tb,),
        in_specs=[row(NA_WIDTH), row(MLA_WIDTH), row(D_MODEL), full(g_na), full(g_mla), full(w_out), full(g_ffn),
                  full(w_r), full(b_r)],
        out_specs=[row(D_MODEL), pl.BlockSpec((tb,) + ROW_TILE, lambda i: (i, 0, 0)), row(LANES), row(LANES),
                   row(LANES), pl.BlockSpec((1, LANES), lambda i: (0, 0))],
        out_shape=[jax.ShapeDtypeStruct((rt, D_MODEL), _F32), jax.ShapeDtypeStruct((rt,) + ROW_TILE, _F32),
                   jax.ShapeDtypeStruct((rt, LANES), _F32), jax.ShapeDtypeStruct((rt, LANES), _F32),
                   jax.ShapeDtypeStruct((rt, LANES), _F32), jax.ShapeDtypeStruct((1, LANES), _F32)],
        scratch_shapes=[pltpu.VMEM((1, LANES), _F32)],
        compiler_params=_cparams(("arbitrary",)),
        name="attn_out_router",
    )(o_na, o_mla, h, g_na, g_mla, w_out, g_ffn, w_r, b_r)


def _dest_kernel(idx_ref, pos_ref, pstart_ref, dest_ref, src_ref, *, n_valid, trash0):
    i = pl.program_id(0)
    tb = idx_ref.shape[0]
    lane_i = lax.broadcasted_iota(jnp.int32, (tb, LANES), 1)
    lane = lane_i.astype(_F32)
    row = lax.broadcasted_iota(jnp.int32, (tb, 1), 0) + i * tb
    valid = row < n_valid
    idx = idx_ref[...]
    out = pos_ref[...]
    for kk in range(TOP_K):
        start = jnp.sum(jnp.where(lane == idx[:, kk:kk + 1], pstart_ref[...], 0.0), axis=-1, keepdims=True)
        out = jnp.where(lane == kk, out + start, out)
    slot = out.astype(jnp.int32)
    dest_ref[...] = jnp.where(valid, slot, trash0 + (row - n_valid) * TOP_K + lane_i)[:, :TOP_K]
    src_ref[...] = jnp.where(valid, slot, 0)[:, :TOP_K]


def _dest(idx, pos, pstart, n_valid, trash0):
    rt = idx.shape[0]
    tb = ROW_BLOCK
    row = pl.BlockSpec((tb, LANES), lambda i: (i, 0))
    return pl.pallas_call(
        functools.partial(_dest_kernel, n_valid=n_valid, trash0=trash0),
        grid=(rt // tb,),
        in_specs=[row, row, pl.BlockSpec((1, LANES), lambda i: (0, 0))],
        out_specs=[pl.BlockSpec((tb, TOP_K), lambda i: (i, 0))] * 2,
        out_shape=[jax.ShapeDtypeStruct((rt, TOP_K), jnp.int32)] * 2,
        compiler_params=_cparams(("parallel",)),
        name="moe_dest",
    )(idx, pos, pstart)


def _row_copies_wait(ref, n_rows, sem):
    pltpu.make_async_copy(ref.at[pl.ds(0, n_rows)], ref.at[pl.ds(0, n_rows)], sem).wait()


def _dispatch_kernel(free_ref, dest_ref, m_ref, xs_ref, zero_sc, sem_free, sem_rows):
    i = pl.program_id(0)
    tb = m_ref.shape[0]
    n_free = free_ref.shape[0]

    @pl.when(i == 0)
    def _():
        zero_sc[...] = jnp.zeros(zero_sc.shape, zero_sc.dtype)

        def fill(j, carry):
            for u in range(2):
                pltpu.make_async_copy(zero_sc, xs_ref.at[free_ref[2 * j + u]], sem_free).start(priority=u)
            return carry

        lax.fori_loop(0, n_free // 2, fill, 0)
        _row_copies_wait(xs_ref, n_free, sem_free)

    def body(t, carry):
        for kk in range(TOP_K):
            d = dest_ref[0, 0, t * TOP_K + kk]
            pltpu.make_async_copy(m_ref.at[t], xs_ref.at[d], sem_rows).start(priority=kk % 2)
        return carry

    lax.fori_loop(0, tb, body, 0)
    _row_copies_wait(xs_ref, tb * TOP_K, sem_rows)


def _dispatch(free_slots, dest3, m, p_rows):
    rt = m.shape[0]
    tb = ROW_BLOCK
    return pl.pallas_call(
        _dispatch_kernel,
        grid_spec=pltpu.PrefetchScalarGridSpec(
            num_scalar_prefetch=1,
            grid=(rt // tb,),
            in_specs=[pl.BlockSpec((1, 1, tb * TOP_K), lambda i, f: (i, 0, 0), memory_space=pltpu.SMEM),
                      pl.BlockSpec((tb,) + ROW_TILE, lambda i, f: (i, 0, 0))],
            out_specs=pl.BlockSpec(memory_space=pl.ANY),
            scratch_shapes=[pltpu.VMEM(ROW_TILE, _F32), pltpu.SemaphoreType.DMA(()),
                            pltpu.SemaphoreType.DMA(())]),
        out_shape=jax.ShapeDtypeStruct((p_rows,) + ROW_TILE, _F32),
        compiler_params=_cparams(("arbitrary",)),
        name="moe_dispatch",
    )(free_slots, dest3, m)


def _ffn_kernel(be_ref, nu_ref, x_ref, wu_ref, bu_ref, wd_ref, bd_ref, y_ref, wu_sc, wd_sc):
    i = pl.program_id(0)

    @pl.when(i < nu_ref[0])
    def _():
        prev = be_ref[jnp.maximum(i - 1, 0)]

        @pl.when((i == 0) | (be_ref[i] != prev))
        def _():
            wu_sc[...] = wu_ref[0, 0].astype(_BF16)
            wd_sc[...] = wd_ref[0, 0].astype(_BF16)

        x = x_ref[...].reshape(x_ref.shape[0], D_MODEL)
        h = _dot(x.astype(_BF16), wu_sc[...]) + bu_ref[0, 0]
        gate = jnp.minimum(h[:, :D_FF], SWIGLU_LIMIT)
        up = jnp.clip(h[:, D_FF:], -SWIGLU_LIMIT, SWIGLU_LIMIT)
        glu = (0.5 * gate) * (1.0 + jnp.tanh(gate * (0.5 * SWIGLU_ALPHA)))
        y = _dot(((up + 1.0) * glu).astype(_BF16), wd_sc[...]) + bd_ref[0, 0]
        y_ref[...] = y.reshape(y_ref.shape)

    @pl.when(i >= nu_ref[0])
    def _():
        y_ref[...] = jnp.zeros(y_ref.shape, y_ref.dtype)


def _ffn(block_e, n_used, xs, w_up, b_up, w_down, b_down, n_blocks, layer):
    bm = EXPERT_BLOCK
    return pl.pallas_call(
        _ffn_kernel,
        grid_spec=pltpu.PrefetchScalarGridSpec(
            num_scalar_prefetch=2,
            grid=(n_blocks,),
            in_specs=[pl.BlockSpec((bm,) + ROW_TILE, lambda i, be, nu: (i, 0, 0)),
                      pl.BlockSpec((1, 1, D_MODEL, 2 * D_FF), lambda i, be, nu: (layer, be[i], 0, 0)),
                      pl.BlockSpec((1, 1, 1, 2 * D_FF), lambda i, be, nu: (layer, be[i], 0, 0)),
                      pl.BlockSpec((1, 1, D_FF, D_MODEL), lambda i, be, nu: (layer, be[i], 0, 0)),
                      pl.BlockSpec((1, 1, 1, D_MODEL), lambda i, be, nu: (layer, be[i], 0, 0))],
            out_specs=pl.BlockSpec((bm,) + ROW_TILE, lambda i, be, nu: (i, 0, 0)),
            scratch_shapes=[pltpu.VMEM((D_MODEL, 2 * D_FF), _BF16), pltpu.VMEM((D_FF, D_MODEL), _BF16)]),
        out_shape=jax.ShapeDtypeStruct((n_blocks * bm,) + ROW_TILE, _F32),
        compiler_params=_cparams(("arbitrary",)),
        name="moe_ffn",
    )(block_e, n_used, xs, w_up, b_up, w_down, b_down)


def _combine_kernel(dest_ref, gate_ref, h1_ref, g_ref, y_ref, o_ref, buf_sc, sem, *, final):
    tb = h1_ref.shape[0]

    def body(t, carry):
        for kk in range(TOP_K):
            d = dest_ref[0, 0, t * TOP_K + kk]
            pltpu.make_async_copy(y_ref.at[d], buf_sc.at[kk, t], sem).start(priority=kk % 2)
        return carry

    lax.fori_loop(0, tb, body, 0)
    _row_copies_wait(y_ref, tb * TOP_K, sem)
    gate = gate_ref[...]
    out = h1_ref[...]
    for kk in range(TOP_K):
        out = out + gate[:, kk:kk + 1] * buf_sc[kk].reshape(tb, D_MODEL)
    if final:
        out = _rms(out, g_ref[...])
    o_ref[...] = out


def _combine(dest4, gate, h1, g_final, y, final, tb, blk0, nblk):
    dest3 = dest4[blk0 * tb:(blk0 + nblk) * tb].reshape(nblk, 1, tb * TOP_K)
    return pl.pallas_call(
        functools.partial(_combine_kernel, final=final),
        grid=(nblk,),
        in_specs=[pl.BlockSpec((1, 1, tb * TOP_K), lambda i: (i, 0, 0), memory_space=pltpu.SMEM),
                  pl.BlockSpec((tb, LANES), lambda i: (blk0 + i, 0)),
                  pl.BlockSpec((tb, D_MODEL), lambda i: (blk0 + i, 0)),
                  pl.BlockSpec((1, D_MODEL), lambda i: (0, 0)),
                  pl.BlockSpec(memory_space=pl.ANY)],
        out_specs=pl.BlockSpec((tb, D_MODEL), lambda i: (i, 0)),
        out_shape=jax.ShapeDtypeStruct((nblk * tb, D_MODEL), _F32),
        scratch_shapes=[pltpu.VMEM((TOP_K, tb) + ROW_TILE, _F32), pltpu.SemaphoreType.DMA(())],
        compiler_params=_cparams(("arbitrary",)),
        name="moe_combine",
    )(dest3, gate, h1, g_final, y)


def _make_layout(seq_tokens, groups):
    n_seq = len(seq_tokens)
    tok_off = np.concatenate([[0], np.cumsum(seq_tokens)]).astype(np.int64)
    nt = int(tok_off[-1])
    n_valid = nt + n_seq * N_META
    rt = -(-n_valid // ROW_BLOCK) * ROW_BLOCK
    pos = np.zeros((rt,), np.float32)
    cidx, var, midx = [], [], []
    for s, n in enumerate(seq_tokens):
        assert n % NA_QBLOCK == 0 and n // NA_QBLOCK >= NA_KBLOCKS
        pos[tok_off[s]:tok_off[s] + n] = N_META + np.arange(n)
        pos[nt + s * N_META:nt + (s + 1) * N_META] = np.arange(N_META)
        nb = n // NA_QBLOCK
        b0 = int(tok_off[s]) // NA_QBLOCK
        for b in range(nb):
            cidx.append(b0 + min(max(b, 1), nb - 2))
            var.append(0 if b == 0 else (2 if b == nb - 1 else 1))
            midx.append(nt // N_META + s)
    return {
        "seq_tokens": tuple(seq_tokens), "groups": tuple(groups), "tok_off": tuple(int(v) for v in tok_off),
        "n_tok_total": nt, "n_valid": n_valid, "rt": rt, "pos": pos,
        "na_cidx": jnp.asarray(cidx, jnp.int32), "na_var": jnp.asarray(var, jnp.int32),
        "na_midx": jnp.asarray(midx, jnp.int32),
        "seq_midx": jnp.asarray([nt // N_META + s for s in range(n_seq)], jnp.int32),
    }


def _rope_tables(pos):
    freqs = jnp.power(ROPE_THETA, -jnp.arange(0, QK_ROPE, 2, dtype=_F32) / QK_ROPE)
    ang = jnp.asarray(pos)[:, None] * freqs[None, :]
    cos, sin = jnp.cos(ang), jnp.sin(ang)
    rt = pos.shape[0]
    pad = LANES - QK_NOPE - QK_ROPE
    cos_t = jnp.concatenate([jnp.ones((rt, QK_NOPE), _F32), cos, cos, jnp.zeros((rt, pad), _F32)], axis=1)
    sin_t = jnp.concatenate([jnp.zeros((rt, QK_NOPE), _F32), sin, sin, jnp.zeros((rt, pad), _F32)], axis=1)
    return cos_t, sin_t


def _layer_weights(w_in, w_uq, w_ukv):
    half = QK_ROPE // 2
    s2 = 3 * NA_WIDTH
    kr_cols = w_in[:, s2 + Q_LORA + KV_LORA:]
    w_in_p = jnp.concatenate([w_in[:, :NA_WIDTH] * (NA_HEAD_DIM ** -0.5), w_in[:, NA_WIDTH:s2 + Q_LORA + KV_LORA],
                              kr_cols, jnp.zeros((D_MODEL, LANES - QK_ROPE), _F32)], axis=1).astype(_BF16)
    dq = QK_NOPE + QK_ROPE
    wq = w_uq.reshape(Q_LORA, MLA_HEADS, dq)
    zq = jnp.zeros((Q_LORA, MLA_HEADS, LANES - dq), _F32)
    q_plain = jnp.concatenate([wq, zq], axis=2)
    q_rot = jnp.concatenate([jnp.zeros((Q_LORA, MLA_HEADS, QK_NOPE), _F32), -wq[:, :, QK_NOPE + half:],
                             wq[:, :, QK_NOPE:QK_NOPE + half], zq], axis=2)
    w_q_p = jnp.concatenate([q_plain.reshape(Q_LORA, -1), q_rot.reshape(Q_LORA, -1)], axis=1).astype(_BF16)
    wkv = w_ukv.reshape(KV_LORA, MLA_HEADS, QK_NOPE + V_HEAD)
    k_plain = jnp.concatenate([wkv[:, :, :QK_NOPE], jnp.zeros((KV_LORA, MLA_HEADS, LANES - QK_NOPE), _F32)], axis=2)
    wv = wkv[:, :, QK_NOPE:].reshape(KV_LORA, N_PAIRS, 2, V_HEAD)
    zv = jnp.zeros((KV_LORA, N_PAIRS, LANES - V_HEAD), _F32)
    v_even = jnp.concatenate([wv[:, :, 0], zv], axis=2)
    v_odd = jnp.concatenate([zv, wv[:, :, 1]], axis=2)
    v_plain = jnp.stack([v_even, v_odd], axis=2)
    w_kv_p = jnp.concatenate([k_plain.reshape(KV_LORA, -1), v_plain.reshape(KV_LORA, -1)], axis=1).astype(_BF16)
    return w_in_p, w_q_p, w_kv_p


def _const_tables():
    half = QK_ROPE // 2
    width = MLA_HEADS * HEAD_TILE
    r_plain = np.zeros((LANES, width), np.float32)
    r_rot = np.zeros((LANES, width), np.float32)
    vone = np.zeros((1, width), np.float32)
    for h in range(MLA_HEADS):
        base = h * HEAD_TILE + QK_NOPE
        for j in range(QK_ROPE):
            r_plain[j, base + j] = 1.0
        for j in range(half):
            r_rot[half + j, base + j] = -1.0
            r_rot[j, base + half + j] = 1.0
        vone[0, h * HEAD_TILE + (V_HEAD if h % 2 == 0 else 0)] = 1.0
    w_kr_p = jnp.asarray(np.concatenate([r_plain, r_rot], axis=1), _BF16)
    return w_kr_p, jnp.asarray(vone)


def _moe_plan(counts, n_blocks, n_free):
    counts = counts[0, :N_EXPERTS].astype(jnp.int32)
    padded = (counts + EXPERT_BLOCK - 1) // EXPERT_BLOCK * EXPERT_BLOCK
    pend = jnp.cumsum(padded)
    pstart = pend - padded
    first_row = jnp.arange(n_blocks, dtype=jnp.int32) * EXPERT_BLOCK
    block_e = jnp.minimum(jnp.sum((pend[None, :] <= first_row[:, None]).astype(jnp.int32), axis=1), N_EXPERTS - 1)
    n_used = (pend[-1:] // EXPERT_BLOCK).astype(jnp.int32)
    pstart_row = jnp.zeros((1, LANES), _F32).at[0, :N_EXPERTS].set(pstart.astype(_F32))
    gap = jnp.concatenate([padded - counts, (n_blocks * EXPERT_BLOCK - pend[-1:])])
    gap_end = jnp.cumsum(gap)
    gap_first = jnp.concatenate([pstart + counts, pend[-1:]])
    j = jnp.arange(n_free, dtype=jnp.int32)
    seg = jnp.sum((gap_end[None, :] <= j[:, None]).astype(jnp.int32), axis=1)
    free_slots = (gap_first[seg] + j - (gap_end - gap)[seg]).astype(jnp.int32)
    return pstart_row, block_e, n_used, free_slots


def _forward(h, layout, meta_tokens, g_attn, w_in, g_q, w_uq, g_kv, w_ukv, rpb, g_out_na, g_out_mla, w_out,
             g_ffn, w_router, b_router, w_up, b_up, w_down, b_down, g_final):
    del meta_tokens
    depth = w_in.shape[0]
    rt = layout["rt"]
    n_valid = layout["n_valid"]
    cos_t, sin_t = _rope_tables(layout["pos"])
    w_kr_p, vone = _const_tables()
    n_assign = n_valid * TOP_K
    n_blocks = -(-n_assign // EXPERT_BLOCK) + N_EXPERTS
    trash0 = n_blocks * EXPERT_BLOCK
    p_rows = trash0 + (rt - n_valid) * TOP_K
    n_free = trash0 - n_assign
    row2 = lambda a: a.reshape(1, -1)
    for l in range(depth):
        w_in_p, w_q_p, w_kv_p = _layer_weights(w_in[l], w_uq[l], w_ukv[l])
        qna, kna, vna, qm, km, vm, kmt = _attn_in(h, cos_t, sin_t, row2(g_attn[l]), w_in_p, row2(g_q[l]), w_q_p,
                                             row2(g_kv[l]), w_kv_p, w_kr_p, vone)
        o_na = _na_attention(qna, kna, vna, _na_bias(rpb[l]), layout)
        o_mla = _mla_attention(qm, kmt, km, vm, layout)
        w_r32 = jnp.concatenate([w_router[l], jnp.zeros((D_MODEL, LANES - N_EXPERTS), _F32)], axis=1)
        w_r_hi = w_r32.astype(_BF16)
        w_r = jnp.concatenate([w_r_hi, (w_r32 - w_r_hi.astype(_F32)).astype(_BF16)], axis=1)
        b_r = jnp.concatenate([b_router[l], jnp.full((LANES - N_EXPERTS,), NEG_BIG, _F32)]).reshape(1, LANES)
        h1, m, idx, pos, gate, counts = _attn_out(o_na, o_mla, h, row2(g_out_na[l]), row2(g_out_mla[l]),
                                                  w_out[l].astype(_BF16), row2(g_ffn[l]), w_r, b_r, n_valid)
        pstart_row, block_e, n_used, free_slots = _moe_plan(counts, n_blocks, n_free)
        dest4, dest4c = _dest(idx, pos, pstart_row, n_valid, trash0)
        xs = _dispatch(free_slots, dest4.reshape(rt // ROW_BLOCK, 1, ROW_BLOCK * TOP_K), m, p_rows)
        y = _ffn(block_e, n_used, xs, w_up, b_up.reshape(depth, N_EXPERTS, 1, -1), w_down,
                 b_down.reshape(depth, N_EXPERTS, 1, -1), n_blocks, l)
        if l < depth - 1:
            h = _combine(dest4c, gate, h1, row2(g_final), y, False, ROW_BLOCK, 0, rt // ROW_BLOCK)
    outs = []
    for seq0, n_seq, n_tok in layout["groups"]:
        tb = min(FINAL_BLOCK, n_tok)
        outs.append(_combine(dest4c, gate, h1, row2(g_final), y, True, tb, layout["tok_off"][seq0] // tb,
                             n_seq * n_tok // tb))
    return outs


def kernel(x_prompt, x_sample, meta_tokens, g_attn, w_in, g_q, w_uq, g_kv, w_ukv, rpb, g_out_na, g_out_mla, w_out,
           g_ffn, w_router, b_router, w_up, b_up, w_down, b_down, g_final):
    bp, lp, _ = x_prompt.shape
    bs, ls, _ = x_sample.shape
    seq_tokens = [lp] * bp + [ls] * bs
    layout = _make_layout(seq_tokens, [(0, bp, lp), (bp, bs, ls)])
    n_seq = len(seq_tokens)
    rt, n_valid = layout["rt"], layout["n_valid"]
    meta = jnp.broadcast_to(meta_tokens[None], (n_seq, N_META, D_MODEL)).reshape(n_seq * N_META, D_MODEL)
    h = jnp.concatenate([x_prompt.reshape(bp * lp, D_MODEL), x_sample.reshape(bs * ls, D_MODEL), meta,
                         jnp.zeros((rt - n_valid, D_MODEL), _F32)], axis=0)
    out = _forward(h, layout, meta_tokens, g_attn, w_in, g_q, w_uq, g_kv, w_ukv, rpb, g_out_na, g_out_mla, w_out,
                   g_ffn, w_router, b_router, w_up, b_up, w_down, b_down, g_final)
    return (out[0].reshape(bp, lp, D_MODEL), out[1].reshape(bs, ls, D_MODEL))
```
